```python
import math
import jax, jax.numpy as jnp
from jax import lax
import numpy as np

D_MODEL = 2048
BATCH = 4
SEQ = 4096
DEPTH = 2

GRID_W = 64
CTX_LEN = 256
N_MIXERS = 2
N_RWKV_LAYERS = (DEPTH + N_MIXERS - 1) // N_MIXERS
N_MLA_LAYERS = DEPTH // N_MIXERS
RMS_EPS = 1e-6
RWKV_HEAD = 64
RWKV_HEADS = D_MODEL // RWKV_HEAD
DECAY_LORA = 96
ICLR_LORA = 96
GATE_LORA = 256
GN_EPS = 64e-5
MLA_HEADS = 16
Q_LORA = 512
KV_LORA = 512
QK_NOPE = 128
QK_ROPE = 64
V_HEAD = 128
MLA_SCALE = (QK_NOPE + QK_ROPE) ** -0.5
ROPE_THETA = 10000.0
ROPE_PAIRS_PER_AXIS = QK_ROPE // 4
Q_BLOCK = 128
D_FF = 5632
CONV_W = 3

kernel_name = "hybrid_rwkv7_mla_convffn_dit"


def rmsnorm(x, g):
    xf = x.astype(jnp.float32)
    y = xf * lax.rsqrt(jnp.mean(xf * xf, axis=-1, keepdims=True) + RMS_EPS)
    return y.astype(x.dtype) * g


def modulate(h, shift, scale):
    return h * (1.0 + scale) + shift


def shift_prev_next(h):
    z = jnp.zeros_like(h[:, :1])
    prev = jnp.concatenate([z, h[:, :-1]], axis=1)
    nxt = jnp.concatenate([h[:, 1:], z], axis=1)
    return prev, nxt


def l2_normalize(t):
    tf = t.astype(jnp.float32)
    n = jnp.sqrt(jnp.sum(tf * tf, axis=-1, keepdims=True))
    return (tf / jnp.maximum(n, 1e-12)).astype(t.dtype)


def rwkv_project(h, mu, wr, wk, wv, w0, w1, w2, a0, a1, a2, g1, g2, k_k, k_a):
    B, L, _ = h.shape
    heads = lambda t: t.reshape(B, L, RWKV_HEADS, RWKV_HEAD)
    prev, nxt = shift_prev_next(h)
    xx = 0.5 * (prev + nxt) - h
    xr, xw, xk, xv, xa, xg = (h + xx * mu[j] for j in range(6))
    r = heads(xr @ wr)
    k = xk @ wk
    v = heads(xv @ wv)
    g = jax.nn.sigmoid(xg @ g1) @ g2
    kk = l2_normalize(heads(k * k_k))
    dirs = []
    for d in range(2):
        w_log = -jax.nn.softplus(-(w0[d] + jnp.tanh(xw @ w1[d]) @ w2[d])) - 0.5
        decay = jnp.exp(-jnp.exp(w_log.astype(jnp.float32)))
        a = jax.nn.sigmoid(a0[d] + (xa @ a1[d]) @ a2[d])
        kd = heads(k * (1.0 + (a - 1.0) * k_a))
        dirs.append((heads(decay), kd, kk * heads(a)))
    return r, v, g, kk, dirs


def rwkv7_scan(S0, r, w, k, v, a, b, reverse):
    xs = tuple(jnp.moveaxis(t.astype(jnp.float32), 1, 0) for t in (r, w, k, v, a, b))

    def step(S, inp):
        rt, wt, kt, vt, at, bt = inp
        sa = jnp.einsum('bhvk,bhk->bhv', S, at)
        S = S * wt[:, :, None, :] + sa[..., :, None] * bt[:, :, None, :] + vt[..., :, None] * kt[:, :, None, :]
        return S, jnp.einsum('bhvk,bhk->bhv', S, rt)

    S, ys = lax.scan(step, S0, xs, reverse=reverse)
    return S, jnp.moveaxis(ys, 0, 1)


def rwkv_readout(y, r, ks, v, g, r_k, gn_w, gn_b, wo):
    B, L, H, N = r.shape
    mean = jnp.mean(y, axis=-1, keepdims=True)
    var = jnp.mean((y - mean) ** 2, axis=-1, keepdims=True)
    yn = ((y - mean) * lax.rsqrt(var + GN_EPS)).astype(r.dtype).reshape(B, L, H * N) * gn_w + gn_b
    bonus = (jnp.sum(r * ks[0] * r_k, axis=-1, keepdims=True)
             + jnp.sum(r * ks[1] * r_k, axis=-1, keepdims=True)) * v
    return ((yn + bonus.reshape(B, L, H * N)) * g) @ wo


def rwkv_mixer(h_ctx, h_lat, need_ctx, mu, wr, wk, wv, wo, w0, w1, w2, a0, a1, a2, g1, g2,
               k_k, k_a, r_k, gn_w, gn_b):
    proj = lambda h: rwkv_project(h, mu, wr, wk, wv, w0, w1, w2, a0, a1, a2, g1, g2, k_k, k_a)
    rc, vc, gc, kkc, dc = proj(h_ctx)
    rl, vl, gl, kkl, dl = proj(h_lat)
    B = h_lat.shape[0]
    S0 = jnp.zeros((B, RWKV_HEADS, RWKV_HEAD, RWKV_HEAD), jnp.float32)
    ys_c, ys_l = [], []
    for d, rev in enumerate((False, True)):
        wc, kc, bc = dc[d]
        wl, kl, bl = dl[d]
        S_ctx, yc = rwkv7_scan(S0, rc, wc, kc, vc, -kkc, bc, rev)
        _, yl = rwkv7_scan(S_ctx, rl, wl, kl, vl, -kkl, bl, rev)
        ys_c.append(yc)
        ys_l.append(yl)
    out_l = rwkv_readout(ys_l[0] + ys_l[1], rl, (dl[0][1], dl[1][1]), vl, gl, r_k, gn_w, gn_b, wo)
    out_c = None
    if need_ctx:
        out_c = rwkv_readout(ys_c[0] + ys_c[1], rc, (dc[0][1], dc[1][1]), vc, gc, r_k, gn_w, gn_b, wo)
    return out_l, out_c


def rope_half(x, cos, sin):
    x1, x2 = jnp.split(x, 2, axis=-1)
    return jnp.concatenate([x1 * cos - x2 * sin, x2 * cos + x1 * sin], axis=-1)


def axial_rope(x, cos_r, sin_r, cos_c, sin_c):
    xr, xc = jnp.split(x, 2, axis=-1)
    return jnp.concatenate([rope_half(xr, cos_r, sin_r), rope_half(xc, cos_c, sin_c)], axis=-1)


def mla_project(h, wdown, qnorm, kvnorm, wuq, wukv, rope):
    B, L, _ = h.shape
    c_q, c_kv, k_rope = jnp.split(h @ wdown, [Q_LORA, Q_LORA + KV_LORA], axis=-1)
    q = (rmsnorm(c_q, qnorm) @ wuq).reshape(B, L, MLA_HEADS, QK_NOPE + QK_ROPE)
    kv = (rmsnorm(c_kv, kvnorm) @ wukv).reshape(B, L, MLA_HEADS, QK_NOPE + V_HEAD)
    q_nope, q_rope = jnp.split(q, [QK_NOPE], axis=-1)
    k_nope, v = jnp.split(kv, [QK_NOPE], axis=-1)
    if rope is not None:
        q_rope = axial_rope(q_rope, *(t[:, None, :] for t in rope))
        k_rope = axial_rope(k_rope, *rope)
    return q_nope, q_rope, k_nope, k_rope, v


def mla_attend(q_nope, q_rope, k_nope, k_rope, v):
    s = (jnp.einsum('bqhd,bkhd->bhqk', q_nope, k_nope)
         + jnp.einsum('bqhr,bkr->bhqk', q_rope, k_rope))
    p = jax.nn.softmax(s.astype(jnp.float32) * MLA_SCALE, axis=-1).astype(v.dtype)
    return jnp.einsum('bhqk,bkhd->bqhd', p, v)


def mla_mixer(h_ctx, h_lat, need_ctx, rope, wdown, qnorm, kvnorm, wuq, wukv, wo):
    qn_c, qr_c, kn_c, kr_c, v_c = mla_project(h_ctx, wdown, qnorm, kvnorm, wuq, wukv, None)
    qn_l, qr_l, kn_l, kr_l, v_l = mla_project(h_lat, wdown, qnorm, kvnorm, wuq, wukv, rope)
    kn = jnp.concatenate([kn_c, kn_l], axis=1)
    kr = jnp.concatenate([kr_c, kr_l], axis=1)
    v = jnp.concatenate([v_c, v_l], axis=1)
    B, L = h_lat.shape[0], h_lat.shape[1]
    nb = L // Q_BLOCK
    to_blocks = lambda t: jnp.moveaxis(t.reshape(B, nb, Q_BLOCK, *t.shape[2:]), 1, 0)
    o = lax.map(lambda qb: mla_attend(qb[0], qb[1], kn, kr, v), (to_blocks(qn_l), to_blocks(qr_l)))
    o = jnp.moveaxis(o, 0, 1).reshape(B, L, MLA_HEADS * V_HEAD)
    out_l = o @ wo
    out_c = None
    if need_ctx:
        oc = mla_attend(qn_c, qr_c, kn_c, kr_c, v_c)
        out_c = oc.reshape(B, h_ctx.shape[1], MLA_HEADS * V_HEAD) @ wo
    return out_l, out_c


def conv_ffn(h, wup, conv_w, conv_b, wdown):
    gate, val = jnp.split(h @ wup, 2, axis=-1)
    prev, nxt = shift_prev_next(gate)
    gate = prev * conv_w[0] + gate * conv_w[1] + nxt * conv_w[2] + conv_b
    return (jax.nn.silu(gate) * val) @ wdown


def setup_inputs(seed: int = 0) -> dict:
    key = jax.random.key(seed)
    ks = iter(jax.random.split(key, 48))
    f32 = jnp.float32
    D, H, N, NA, NB = D_MODEL, RWKV_HEADS, RWKV_HEAD, N_RWKV_LAYERS, N_MLA_LAYERS

    def nrm(shape, scale):
        return jax.random.normal(next(ks), shape, f32) * scale

    def gain(shape):
        return 1.0 + nrm(shape, 0.02)

    decay_base = -6.0 + 5.0 * jnp.linspace(0.0, 1.0, D, dtype=f32) ** 1.5
    return {
        "x": nrm((BATCH, SEQ, D), 1.0),
        "c": nrm((BATCH, D), 1.0),
        "ctx": nrm((BATCH, CTX_LEN, D), 1.0),
        "c_ctx": nrm((D,), 1.0),
        "ada_w": nrm((DEPTH, D, 6 * D), 0.5 * D ** -0.5),
        "ada_b": nrm((DEPTH, 6 * D), 0.02),
        "norm_g": gain((DEPTH, 2, D)),
        "final_g": gain((D,)),
        "rk_mu": jax.random.uniform(next(ks), (NA, 6, D), f32),
        "rk_wr": nrm((NA, D, D), D ** -0.5),
        "rk_wk": nrm((NA, D, D), D ** -0.5),
        "rk_wv": nrm((NA, D, D), D ** -0.5),
        "rk_wo": nrm((NA, D, D), D ** -0.5),
        "rk_w0": decay_base + nrm((NA, 2, D), 0.1),
        "rk_w1": nrm((NA, 2, D, DECAY_LORA), D ** -0.5),
        "rk_w2": nrm((NA, 2, DECAY_LORA, D), 0.1 * DECAY_LORA ** -0.5),
        "rk_a0": nrm((NA, 2, D), 0.1),
        "rk_a1": nrm((NA, 2, D, ICLR_LORA), D ** -0.5),
        "rk_a2": nrm((NA, 2, ICLR_LORA, D), 0.1 * ICLR_LORA ** -0.5),
        "rk_g1": nrm((NA, D, GATE_LORA), D ** -0.5),
        "rk_g2": nrm((NA, GATE_LORA, D), GATE_LORA ** -0.5),
        "rk_kk": 0.85 + nrm((NA, D), 0.02),
        "rk_ka": gain((NA, D)),
        "rk_rk": nrm((NA, H, N), 0.1),
        "rk_gn_w": gain((NA, D)),
        "rk_gn_b": nrm((NA, D), 0.02),
        "ml_wdown": nrm((NB, D, Q_LORA + KV_LORA + QK_ROPE), D ** -0.5),
        "ml_qnorm": gain((NB, Q_LORA)),
        "ml_kvnorm": gain((NB, KV_LORA)),
        "ml_wuq": nrm((NB, Q_LORA, MLA_HEADS * (QK_NOPE + QK_ROPE)), Q_LORA ** -0.5),
        "ml_wukv": nrm((NB, KV_LORA, MLA_HEADS * (QK_NOPE + V_HEAD)), KV_LORA ** -0.5),
        "ml_wo": nrm((NB, MLA_HEADS * V_HEAD, D), (MLA_HEADS * V_HEAD) ** -0.5),
        "ff_wup": nrm((DEPTH, D, 2 * D_FF), D ** -0.5),
        "ff_conv": nrm((DEPTH, CONV_W, D_FF), 0.5),
        "ff_convb": nrm((DEPTH, D_FF), 0.02),
        "ff_wdown": nrm((DEPTH, D_FF, D), D_FF ** -0.5),
    }


def reference(x, c, ctx, c_ctx, ada_w, ada_b, norm_g, final_g,
              rk_mu, rk_wr, rk_wk, rk_wv, rk_wo, rk_w0, rk_w1, rk_w2, rk_a0, rk_a1, rk_a2,
              rk_g1, rk_g2, rk_kk, rk_ka, rk_rk, rk_gn_w, rk_gn_b,
              ml_wdown, ml_qnorm, ml_kvnorm, ml_wuq, ml_wukv, ml_wo,
              ff_wup, ff_conv, ff_convb, ff_wdown):
    n_lat = x.shape[1]
    rows = n_lat // GRID_W
    row = jnp.broadcast_to(jnp.arange(rows)[:, None], (rows, GRID_W)).reshape(-1)
    col = jnp.broadcast_to(jnp.arange(GRID_W)[None, :], (rows, GRID_W)).reshape(-1)
    inv_freq = jnp.float32(ROPE_THETA) ** (-jnp.arange(ROPE_PAIRS_PER_AXIS, dtype=jnp.float32) / ROPE_PAIRS_PER_AXIS)
    ang_r = row.astype(jnp.float32)[:, None] * inv_freq
    ang_c = col.astype(jnp.float32)[:, None] * inv_freq
    rope = (jnp.cos(ang_r).astype(x.dtype), jnp.sin(ang_r).astype(x.dtype),
            jnp.cos(ang_c).astype(x.dtype), jnp.sin(ang_c).astype(x.dtype))

    s_ctx = ctx
    for i in range(DEPTH):
        last = i == DEPTH - 1
        m_lat = jax.nn.silu(c) @ ada_w[i] + ada_b[i]
        m_ctx = jax.nn.silu(c_ctx) @ ada_w[i] + ada_b[i]
        sh1, sc1, gt1, sh2, sc2, gt2 = (t[:, None, :] for t in jnp.split(m_lat, 6, axis=-1))
        csh1, csc1, cgt1, csh2, csc2, cgt2 = jnp.split(m_ctx, 6, axis=-1)

        h_lat = modulate(rmsnorm(x, norm_g[i, 0]), sh1, sc1)
        h_ctx = modulate(rmsnorm(s_ctx, norm_g[i, 0]), csh1, csc1)
        j = i // N_MIXERS
        if i % N_MIXERS == 0:
            out_l, out_c = rwkv_mixer(h_ctx, h_lat, not last, rk_mu[j], rk_wr[j], rk_wk[j], rk_wv[j], rk_wo[j],
                                      rk_w0[j], rk_w1[j], rk_w2[j], rk_a0[j], rk_a1[j], rk_a2[j],
                                      rk_g1[j], rk_g2[j], rk_kk[j], rk_ka[j], rk_rk[j], rk_gn_w[j], rk_gn_b[j])
        else:
            out_l, out_c = mla_mixer(h_ctx, h_lat, not last, rope, ml_wdown[j], ml_qnorm[j], ml_kvnorm[j],
                                     ml_wuq[j], ml_wukv[j], ml_wo[j])
        x = x + gt1 * out_l
        hf_lat = modulate(rmsnorm(x, norm_g[i, 1]), sh2, sc2)
        x = x + gt2 * conv_ffn(hf_lat, ff_wup[i], ff_conv[i], ff_convb[i], ff_wdown[i])
        if not last:
            s_ctx = s_ctx + cgt1 * out_c
            hf_ctx = modulate(rmsnorm(s_ctx, norm_g[i, 1]), csh2, csc2)
            s_ctx = s_ctx + cgt2 * conv_ffn(hf_ctx, ff_wup[i], ff_conv[i], ff_convb[i], ff_wdown[i])
    return rmsnorm(x, final_g)
```

```python
import functools
import math

import jax
import jax.numpy as jnp
from jax import lax
from jax.experimental import pallas as pl
from jax.experimental.pallas import tpu as pltpu

F32 = jnp.float32
BF16 = jnp.bfloat16

RMS_EPS = 1e-6
GN_EPS = 64e-5
QK_NOPE = 128
QK_ROPE = 64
V_HEAD = 128
GRID_W = 64
ROPE_THETA = 10000.0
LANES = 128
SUBLANES = 8
BF16_ROWS = 16
VMEM_LIMIT = 56 * 1024 * 1024
MOD_ROWS = 8


def _cparams(n_axes):
    return pltpu.CompilerParams(dimension_semantics=("arbitrary",) * n_axes, vmem_limit_bytes=VMEM_LIMIT)


def _pick(m, cands):
    for c in cands:
        if m % c == 0:
            return c
    raise ValueError(f"no tile for {m} in {cands}")


def _round_up(x, m):
    return (x + m - 1) // m * m


class _Rows:
    def __init__(self, batch, n_ctx, n_lat, sub):
        self.batch, self.n_ctx, self.n_lat, self.sub = batch, n_ctx, n_lat, sub
        self.per_batch = n_ctx + n_lat
        self.total = batch * self.per_batch
        self.spb = self.per_batch // sub
        self.ctx_sub = n_ctx // sub

    def info(self, q):
        b = q // self.spb
        w = q - b * self.spb
        if self.ctx_sub:
            idx = jnp.where(w < self.ctx_sub, self.batch, b)
            start = (w == 0) | (w == self.ctx_sub)
            end = (w == self.ctx_sub - 1) | (w == self.spb - 1)
        else:
            idx = b
            start = w == 0
            end = w == self.spb - 1
        return idx, start, end


def _silu(x):
    return x * jax.nn.sigmoid(x)


def _rms(x):
    return x * lax.rsqrt(jnp.mean(x * x, axis=-1, keepdims=True) + RMS_EPS)


def _ada_kernel(c_ref, w_ref, b_ref, o_ref):
    o_ref[...] = jnp.dot(_silu(c_ref[...]), w_ref[...], preferred_element_type=F32) + b_ref[...]


def _ada(cvec, ada_w, ada_b):
    depth, d, n = ada_w.shape
    tn = _pick(n, (512, 256, 128))
    return pl.pallas_call(
        _ada_kernel,
        grid=(depth, n // tn),
        in_specs=[pl.BlockSpec((MOD_ROWS, d), lambda l, j: (0, 0)),
                  pl.BlockSpec((None, d, tn), lambda l, j: (l, 0, j)),
                  pl.BlockSpec((None, 1, tn), lambda l, j: (l, 0, j))],
        out_specs=pl.BlockSpec((None, MOD_ROWS, tn), lambda l, j: (l, 0, j)),
        out_shape=jax.ShapeDtypeStruct((depth, MOD_ROWS, n), F32),
        compiler_params=_cparams(2),
        name="ada_mod",
    )(cvec, ada_w, ada_b.reshape(depth, 1, n))


def _norm_mod_kernel(x_ref, g_ref, m_ref, o_ref, *, rows, tm, d, sh_col, sc_col):
    i = pl.program_id(0)
    nsub = tm // rows.sub
    for s in range(nsub):
        idx, _, _ = rows.info(i * nsub + s)
        sl = slice(s * rows.sub, (s + 1) * rows.sub)
        sh = m_ref[pl.ds(idx, 1), sh_col * d:(sh_col + 1) * d]
        sc = m_ref[pl.ds(idx, 1), sc_col * d:(sc_col + 1) * d]
        y = _rms(x_ref[sl, :]) * g_ref[...]
        o_ref[sl, :] = (y * (1.0 + sc) + sh).astype(o_ref.dtype)


def _norm_mod(x2, g, mods, rows, sh_col, sc_col, out_dtype):
    m, d = x2.shape
    tm = _pick(m, tuple(t for t in (512, 256, 128, 64, 32, 16, 8) if t % rows.sub == 0))
    return pl.pallas_call(
        functools.partial(_norm_mod_kernel, rows=rows, tm=tm, d=d, sh_col=sh_col, sc_col=sc_col),
        grid=(m // tm,),
        in_specs=[pl.BlockSpec((tm, d), lambda i: (i, 0)),
                  pl.BlockSpec((1, d), lambda i: (0, 0)),
                  pl.BlockSpec(mods.shape, lambda i: (0, 0))],
        out_specs=pl.BlockSpec((tm, d), lambda i: (i, 0)),
        out_shape=jax.ShapeDtypeStruct((m, d), out_dtype),
        compiler_params=_cparams(1),
        name="norm_mod",
    )(x2, g.reshape(1, d), mods)


def _final_norm_kernel(x_ref, g_ref, o_ref):
    o_ref[...] = _rms(x_ref[...]) * g_ref[...]


def _final_norm(x2, g):
    m, d = x2.shape
    tm = _pick(m, (512, 256, 128, 64))
    return pl.pallas_call(
        _final_norm_kernel,
        grid=(m // tm,),
        in_specs=[pl.BlockSpec((tm, d), lambda i: (i, 0)), pl.BlockSpec((1, d), lambda i: (0, 0))],
        out_specs=pl.BlockSpec((tm, d), lambda i: (i, 0)),
        out_shape=jax.ShapeDtypeStruct((m, d), F32),
        compiler_params=_cparams(1),
        name="final_norm",
    )(x2, g.reshape(1, d))


def _edge_masks(rows, i, tm):
    nsub = tm // rows.sub
    rid = lax.broadcasted_iota(jnp.int32, (tm, 1), 0)
    is_start = jnp.zeros((tm, 1), jnp.bool_)
    is_end = jnp.zeros((tm, 1), jnp.bool_)
    for s in range(nsub):
        _, st, en = rows.info(i * nsub + s)
        is_start = is_start | (rid == jnp.where(st, s * rows.sub, -1))
        is_end = is_end | (rid == jnp.where(en, (s + 1) * rows.sub - 1, -1))
    return rid, is_start, is_end


def _shifted(cur, prev_row, next_row, rid, is_start, is_end, tm):
    prev = jnp.where(rid == 0, prev_row, pltpu.roll(cur, 1, 0))
    nxt = jnp.where(rid == tm - 1, next_row, pltpu.roll(cur, tm - 1, 0))
    return jnp.where(is_start, 0.0, prev), jnp.where(is_end, 0.0, nxt)


def _rwkv_prep_kernel(x_ref, xp_ref, xn_ref, g_ref, m_ref, h_ref, xx_ref, *, rows, tm, d):
    i = pl.program_id(0)
    nsub = tm // rows.sub
    total_sub = rows.total // rows.sub

    def mod_rows(q):
        idx, _, _ = rows.info(q)
        return m_ref[pl.ds(idx, 1), 0:d], m_ref[pl.ds(idx, 1), d:2 * d]

    def h_of(x, q):
        sh, sc = mod_rows(q)
        return (_rms(x) * g_ref[...]) * (1.0 + sc) + sh

    for s in range(nsub):
        sl = slice(s * rows.sub, (s + 1) * rows.sub)
        h_ref[sl, :] = h_of(x_ref[sl, :], i * nsub + s)
    h = h_ref[...]
    hp = h_of(xp_ref[SUBLANES - 1:SUBLANES, :], jnp.maximum(i * nsub - 1, 0))
    hn = h_of(xn_ref[0:1, :], jnp.minimum((i + 1) * nsub, total_sub - 1))
    rid, is_start, is_end = _edge_masks(rows, i, tm)
    prev, nxt = _shifted(h, hp, hn, rid, is_start, is_end, tm)
    xx_ref[...] = 0.5 * (prev + nxt) - h


def _rwkv_prep(x2, g, mods, rows):
    m, d = x2.shape
    tm = max(_pick(m, (256, 128, 64)), rows.sub)
    r8 = tm // SUBLANES
    nblk8 = m // SUBLANES
    return pl.pallas_call(
        functools.partial(_rwkv_prep_kernel, rows=rows, tm=tm, d=d),
        grid=(m // tm,),
        in_specs=[pl.BlockSpec((tm, d), lambda i: (i, 0)),
                  pl.BlockSpec((SUBLANES, d), lambda i: (jnp.maximum(i * r8 - 1, 0), 0)),
                  pl.BlockSpec((SUBLANES, d), lambda i: (jnp.minimum((i + 1) * r8, nblk8 - 1), 0)),
                  pl.BlockSpec((1, d), lambda i: (0, 0)),
                  pl.BlockSpec(mods.shape, lambda i: (0, 0))],
        out_specs=[pl.BlockSpec((tm, d), lambda i: (i, 0)), pl.BlockSpec((tm, d), lambda i: (i, 0))],
        out_shape=[jax.ShapeDtypeStruct((m, d), F32), jax.ShapeDtypeStruct((m, d), F32)],
        compiler_params=_cparams(1),
        name="rwkv_prep",
    )(x2, x2, x2, g.reshape(1, d), mods)


def _mm_kernel(*refs, na, ne, no, prologue, epilogue, period):
    a_refs = refs[:na]
    w_ref = refs[na]
    e_refs = refs[na + 1:na + 1 + ne]
    o_refs = refs[na + 1 + ne:na + 1 + ne + no]
    a_scr = refs[-1]
    i = pl.program_id(0)
    j = pl.program_id(1)

    @pl.when(j % period == 0)
    def _():
        a_scr[...] = prologue(a_refs, i, j // period)

    acc = jnp.dot(a_scr[...], w_ref[...], preferred_element_type=F32)
    epilogue(acc, e_refs, o_refs, i, j)


def _mm(name, a_arrays, a_specs, prologue, w, e_arrays, e_specs, epilogue, out_shapes, out_specs,
        m, k, n, tm, tn, period=None):
    nj = n // tn
    period = nj if period is None else period
    return pl.pallas_call(
        functools.partial(_mm_kernel, na=len(a_arrays), ne=len(e_arrays), no=len(out_shapes),
                          prologue=prologue, epilogue=epilogue, period=period),
        grid=(m // tm, nj),
        in_specs=list(a_specs) + [pl.BlockSpec((k, tn), lambda i, j: (0, j))] + list(e_specs),
        out_specs=list(out_specs),
        out_shape=list(out_shapes),
        scratch_shapes=[pltpu.VMEM((tm, k), BF16)],
        compiler_params=_cparams(2),
        name=name,
    )(*a_arrays, w, *e_arrays)


def _gated_residual_epilogue(rows, tm):
    nsub = tm // rows.sub

    def epi(acc, e_refs, o_refs, i, j):
        res_ref, m_ref = e_refs
        for s in range(nsub):
            idx, _, _ = rows.info(i * nsub + s)
            sl = slice(s * rows.sub, (s + 1) * rows.sub)
            o_refs[0][sl, :] = res_ref[sl, :] + m_ref[pl.ds(idx, 1), :] * acc[sl, :]
    return epi


def _store_epilogue(acc, e_refs, o_refs, i, j):
    o_refs[0][...] = acc.astype(o_refs[0].dtype)


def _first_input(a_refs, i, grp):
    return a_refs[0][...]


def _proj_residual(name, a_arrays, a_specs, prologue, w, res, mods, rows, gate_col, tm, tn):
    m, d = res.shape
    k = w.shape[0]
    gate_blk = gate_col * (d // tn)
    return _mm(name, a_arrays, a_specs, prologue, w,
               [res, mods],
               [pl.BlockSpec((tm, tn), lambda i, j: (i, j)),
                pl.BlockSpec((MOD_ROWS, tn), lambda i, j: (0, gate_blk + j))],
               _gated_residual_epilogue(rows, tm),
               [jax.ShapeDtypeStruct((m, d), F32)],
               [pl.BlockSpec((tm, tn), lambda i, j: (i, j))],
               m, k, d, tm, tn)[0]


def _softplus(z):
    return jnp.maximum(z, 0.0) + jnp.log(1.0 + jnp.exp(-jnp.abs(z)))


def _scan_kernel(rf, kf, vf, df, af, rr, kr, vr, dr, ar, kk_ref, ka_ref, yf_ref, yr_ref, s_ref, vec_ref,
                 *, tb, n, vchunk):
    j = pl.program_id(0)

    @pl.when(j == 0)
    def _():
        s_ref[...] = jnp.zeros_like(s_ref)

    kkc = kk_ref[...]
    kac = ka_ref[...]
    nvg = n // SUBLANES
    streams = ((rf, kf, vf, df, af, yf_ref), (rr, kr, vr, dr, ar, yr_ref))

    def step(i, carry):
        for dirn, (r_, k_, v_, d_, a_, y_) in enumerate(streams):
            tt = i if dirn == 0 else tb - 1 - i
            kt = k_[tt]
            at = a_[tt]
            kx = kt * kkc
            nrm = jnp.sqrt(jnp.sum(kx * kx, axis=0, keepdims=True))
            kkn = kx / jnp.maximum(nrm, 1e-12)
            vec_ref[dirn, 0] = -kkn
            vec_ref[dirn, 1] = d_[tt]
            vec_ref[dirn, 2] = kkn * at
            vec_ref[dirn, 3] = kt * (1.0 + (at - 1.0) * kac)
            vec_ref[dirn, 4] = r_[tt]

            def row(which, kk):
                return jnp.broadcast_to(vec_ref[dirn, which, kk:kk + 1, :], (SUBLANES, LANES))

            for c in range(nvg // vchunk):
                vgs = [c * vchunk + q for q in range(vchunk)]
                sls = [slice(vg * SUBLANES, (vg + 1) * SUBLANES) for vg in vgs]
                sa = [jnp.zeros((SUBLANES, LANES), F32) for _ in vgs]
                for kk in range(n):
                    ab = row(0, kk)
                    for q, sl in enumerate(sls):
                        sa[q] = sa[q] + s_ref[dirn, kk, sl, :] * ab
                vv = [v_[tt, sl, :] for sl in sls]
                yy = [jnp.zeros((SUBLANES, LANES), F32) for _ in vgs]
                for kk in range(n):
                    wb, bb, kb, rb = row(1, kk), row(2, kk), row(3, kk), row(4, kk)
                    for q, sl in enumerate(sls):
                        snew = s_ref[dirn, kk, sl, :] * wb + sa[q] * bb + vv[q] * kb
                        s_ref[dirn, kk, sl, :] = snew
                        yy[q] = yy[q] + snew * rb
                for q, sl in enumerate(sls):
                    y_[tt, sl, :] = yy[q]
        return carry

    lax.fori_loop(0, tb, step, 0)


def _rwkv_scan(r, k, v, dec0, dec1, a0, a1, kk_c, ka_c, n_ctx, tb):
    lt, n, ln = r.shape
    nb = lt // tb
    nc = n_ctx // tb

    def fwd(j):
        return (j, 0, 0)

    def rev(j):
        return (jnp.where(j < nc, nc - 1 - j, nb - 1 - (j - nc)), 0, 0)

    blk = (tb, n, ln)
    const = pl.BlockSpec((n, ln), lambda j: (0, 0))
    return pl.pallas_call(
        functools.partial(_scan_kernel, tb=tb, n=n, vchunk=4),
        grid=(nb,),
        in_specs=[pl.BlockSpec(blk, fwd)] * 5 + [pl.BlockSpec(blk, rev)] * 5 + [const, const],
        out_specs=[pl.BlockSpec(blk, fwd), pl.BlockSpec(blk, rev)],
        out_shape=[jax.ShapeDtypeStruct((lt, n, ln), F32)] * 2,
        scratch_shapes=[pltpu.VMEM((2, n, n, ln), F32), pltpu.VMEM((2, 5, n, ln), F32)],
        compiler_params=_cparams(1),
        name="rwkv_scan",
    )(r, k, v, dec0, a0, r, k, v, dec1, a1, kk_c, ka_c)


def _readout_kernel(r_ref, k_ref, v_ref, a0_ref, a1_ref, yf_ref, yr_ref, ka_ref, rk_ref, gw_ref, gb_ref, o_ref):
    y = yf_ref[...] + yr_ref[...]
    mean = jnp.mean(y, axis=1, keepdims=True)
    yc = y - mean
    var = jnp.mean(yc * yc, axis=1, keepdims=True)
    yn = (yc * lax.rsqrt(var + GN_EPS)) * gw_ref[...] + gb_ref[...]
    r = r_ref[...]
    k = k_ref[...]
    kd0 = k * (1.0 + (a0_ref[...] - 1.0) * ka_ref[...])
    kd1 = k * (1.0 + (a1_ref[...] - 1.0) * ka_ref[...])
    bonus = jnp.sum(r * kd0 * rk_ref[...], axis=1, keepdims=True) + jnp.sum(r * kd1 * rk_ref[...], axis=1, keepdims=True)
    o_ref[...] = yn + bonus * v_ref[...]


def _rwkv_readout(r, k, v, a0, a1, yf, yr, ka_c, rk_c, gw_c, gb_c, tb):
    lt, n, ln = r.shape
    blk = pl.BlockSpec((tb, n, ln), lambda j: (j, 0, 0))
    const = pl.BlockSpec((n, ln), lambda j: (0, 0))
    return pl.pallas_call(
        _readout_kernel,
        grid=(lt // tb,),
        in_specs=[blk] * 7 + [const] * 4,
        out_specs=blk,
        out_shape=jax.ShapeDtypeStruct((lt, n, ln), F32),
        compiler_params=_cparams(1),
        name="rwkv_readout",
    )(r, k, v, a0, a1, yf, yr, ka_c, rk_c, gw_c, gb_c)


def _ffn_up_kernel(h_ref, hp_ref, hn_ref, wg_ref, wv_ref, cw_ref, cb_ref, o_ref, *, rows, tm):
    i = pl.program_id(0)
    h = h_ref[...]
    wg = wg_ref[...]
    gate = jnp.dot(h, wg, preferred_element_type=F32)
    val = jnp.dot(h, wv_ref[...], preferred_element_type=F32)
    gp = jnp.dot(hp_ref[...], wg, preferred_element_type=F32)[BF16_ROWS - 1:BF16_ROWS, :]
    gn = jnp.dot(hn_ref[...], wg, preferred_element_type=F32)[0:1, :]
    rid, is_start, is_end = _edge_masks(rows, i, tm)
    prev, nxt = _shifted(gate, gp, gn, rid, is_start, is_end, tm)
    conv = prev * cw_ref[0:1, :] + gate * cw_ref[1:2, :] + nxt * cw_ref[2:3, :] + cb_ref[...]
    o_ref[...] = (_silu(conv) * val).astype(o_ref.dtype)


def _ffn_up(hf, wup, conv_w, conv_b, rows, tm, tn):
    m, d = hf.shape
    f = wup.shape[1] // 2
    r8 = tm // BF16_ROWS
    nblk8 = m // BF16_ROWS
    njf = f // tn
    cw = jnp.zeros((SUBLANES, f), F32).at[:conv_w.shape[0]].set(conv_w)
    return pl.pallas_call(
        functools.partial(_ffn_up_kernel, rows=rows, tm=tm),
        grid=(m // tm, njf),
        in_specs=[pl.BlockSpec((tm, d), lambda i, j: (i, 0)),
                  pl.BlockSpec((BF16_ROWS, d), lambda i, j: (jnp.maximum(i * r8 - 1, 0), 0)),
                  pl.BlockSpec((BF16_ROWS, d), lambda i, j: (jnp.minimum((i + 1) * r8, nblk8 - 1), 0)),
                  pl.BlockSpec((d, tn), lambda i, j: (0, j)),
                  pl.BlockSpec((d, tn), lambda i, j: (0, j + njf)),
                  pl.BlockSpec((SUBLANES, tn), lambda i, j: (0, j)),
                  pl.BlockSpec((1, tn), lambda i, j: (0, j))],
        out_specs=pl.BlockSpec((tm, tn), lambda i, j: (i, j)),
        out_shape=jax.ShapeDtypeStruct((m, f), BF16),
        compiler_params=_cparams(2),
        name="ffn_up",
    )(hf, hf, hf, wup, wup, cw, conv_b.reshape(1, f))


def _conv_ffn(x2, g, mods, rows, wup, conv_w, conv_b, wdown, tm):
    m, d = x2.shape
    f = wdown.shape[0]
    hf = _norm_mod(x2, g, mods, rows, 3, 4, BF16)
    act = _ffn_up(hf, wup.astype(BF16), conv_w, conv_b, rows, tm, _pick(f, (512, 256, 128)))
    tmd = _pick(m, tuple(t for t in (512, 256, 128, 64) if t <= tm and t % rows.sub == 0))
    return _proj_residual("ffn_down", [act], [pl.BlockSpec((tmd, f), lambda i, j: (i, 0))],
                          _first_input, wdown.astype(BF16), x2, mods, rows, 5,
                          tmd, _pick(d, (512, 256, 128)))


def _rope(x, cos, sin_signed):
    lane = lax.broadcasted_iota(jnp.int32, x.shape, 1)
    first = (lane & 31) < 16
    swapped = jnp.where(first, pltpu.roll(x, LANES - 16, 1), pltpu.roll(x, 16, 1))
    return x * cos + swapped * sin_signed


def _kv_up_kernel(c_ref, wk_ref, wv_ref, kr_ref, ko_ref, vo_ref, *, hpt):
    c = c_ref[...]
    kn = jnp.dot(c, wk_ref[...], preferred_element_type=F32).astype(BF16)
    vo_ref[...] = jnp.dot(c, wv_ref[...], preferred_element_type=F32).astype(BF16)
    kr = kr_ref[...]
    for h in range(hpt):
        ko_ref[:, h * 2 * LANES:h * 2 * LANES + QK_NOPE] = kn[:, h * QK_NOPE:(h + 1) * QK_NOPE]
        ko_ref[:, h * 2 * LANES + QK_NOPE:(h + 1) * 2 * LANES] = kr


def _attn_kernel(q_ref, k_ref, v_ref, o_ref, *, scale):
    s = lax.dot_general(q_ref[...], k_ref[...], (((1,), (1,)), ((), ())), preferred_element_type=F32) * scale
    p = jnp.exp(s - jnp.max(s, axis=-1, keepdims=True))
    denom = jnp.sum(p, axis=-1, keepdims=True)
    o = jnp.dot(p.astype(BF16), v_ref[...], preferred_element_type=F32)
    o_ref[...] = (o / denom).astype(o_ref.dtype)


def _to_lanes(x, b, lt, h, n):
    y = x.reshape(b, lt, h, n).transpose(1, 3, 0, 2).reshape(lt, n, b * h)
    return jnp.pad(y, ((0, 0), (0, 0), (0, LANES - b * h)))


def _from_lanes(y, b, lt, h, n):
    return y[:, :, :b * h].reshape(lt, n, b, h).transpose(2, 0, 3, 1).reshape(b * lt, h * n)


def _head_const(p_hn, b):
    h, n = p_hn.shape
    y = jnp.broadcast_to(p_hn.T[:, None, :], (n, b, h)).reshape(n, b * h)
    return jnp.pad(y, ((0, 0), (0, LANES - b * h)))


def _rwkv_layer(xa, mods, norm_g, rows, p, tm):
    (mu, wr, wk, wv, wo, w0, w1, w2, a0, a1, a2, g1, g2, k_k, k_a, r_k, gn_w, gn_b) = p
    m, d = xa.shape
    h_heads, n = r_k.shape
    b, lt = rows.batch, rows.per_batch
    glora, dlora, alora = g1.shape[1], w1.shape[2], a1.shape[2]
    gw = _round_up(max(glora, 2 * dlora, 2 * alora), LANES)

    h, xx = _rwkv_prep(xa, norm_g, mods, rows)

    a_specs = [pl.BlockSpec((tm, d), lambda i, j: (i, 0)), pl.BlockSpec((tm, d), lambda i, j: (i, 0)),
               pl.BlockSpec((MOD_ROWS, d), lambda i, j: (0, 0))]

    def mix(a, i, grp):
        return (a[0][...] + a[1][...] * a[2][pl.ds(grp, 1), :]).astype(BF16)

    def pad_rows(t):
        return jnp.zeros((MOD_ROWS, d), F32).at[:t.shape[0]].set(t)

    tn = _pick(d, (512, 256, 128))
    w_rkv = jnp.concatenate([wr, wk, wv], axis=1).astype(BF16)
    rkv = _mm("rwkv_rkv", [h, xx, pad_rows(mu[jnp.array([0, 2, 3])])], a_specs, mix, w_rkv, [], [],
              _store_epilogue,
              [jax.ShapeDtypeStruct((m, 3 * d), F32)], [pl.BlockSpec((tm, tn), lambda i, j: (i, j))],
              m, d, 3 * d, tm, tn, period=d // tn)[0]

    def pad_cols(t):
        return jnp.pad(t, ((0, 0), (0, gw - t.shape[1])))

    w_lora = jnp.concatenate([pad_cols(g1), pad_cols(jnp.concatenate([w1[0], w1[1]], axis=1)),
                              pad_cols(jnp.concatenate([a1[0], a1[1]], axis=1))], axis=1).astype(BF16)

    def lora_act(acc, e, o, i, j):
        o[0][...] = jnp.where(j == 0, jax.nn.sigmoid(acc), jnp.where(j == 1, jnp.tanh(acc), acc)).astype(BF16)

    hid = _mm("rwkv_lora1", [h, xx, pad_rows(mu[jnp.array([5, 1, 4])])], a_specs, mix, w_lora, [], [], lora_act,
              [jax.ShapeDtypeStruct((m, 3 * gw), BF16)], [pl.BlockSpec((tm, gw), lambda i, j: (i, j))],
              m, d, 3 * gw, tm, gw, period=1)[0]

    def hid_spec(grp):
        return [pl.BlockSpec((tm, gw), lambda i, j: (i, grp))]

    ident = _first_input
    g2p = jnp.zeros((gw, d), F32).at[:glora].set(g2).astype(BF16)
    gate = _mm("rwkv_gate", [hid], hid_spec(0), ident, g2p, [], [],
               _store_epilogue,
               [jax.ShapeDtypeStruct((m, d), F32)], [pl.BlockSpec((tm, tn), lambda i, j: (i, j))],
               m, gw, d, tm, tn)[0]

    def two_dir(t2, lora):
        z = jnp.zeros((gw, 2 * d), F32)
        z = z.at[:lora, :d].set(t2[0]).at[lora:2 * lora, d:].set(t2[1])
        return z.astype(BF16)

    bias_spec = [pl.BlockSpec((1, tn), lambda i, j: (0, j))]

    def decay_epi(acc, e, o, i, j):
        w_log = -_softplus(-(e[0][...] + acc)) - 0.5
        o[0][...] = jnp.exp(-jnp.exp(w_log))

    dec = _mm("rwkv_decay", [hid], hid_spec(1), ident, two_dir(w2, dlora), [w0.reshape(1, 2 * d)], bias_spec,
              decay_epi, [jax.ShapeDtypeStruct((m, 2 * d), F32)], [pl.BlockSpec((tm, tn), lambda i, j: (i, j))],
              m, gw, 2 * d, tm, tn)[0]

    def iclr_epi(acc, e, o, i, j):
        o[0][...] = jax.nn.sigmoid(e[0][...] + acc)

    icl = _mm("rwkv_iclr", [hid], hid_spec(2), ident, two_dir(a2, alora), [a0.reshape(1, 2 * d)], bias_spec,
              iclr_epi, [jax.ShapeDtypeStruct((m, 2 * d), F32)], [pl.BlockSpec((tm, tn), lambda i, j: (i, j))],
              m, gw, 2 * d, tm, tn)[0]

    tl = functools.partial(_to_lanes, b=b, lt=lt, h=h_heads, n=n)
    r_l, k_l, v_l = tl(rkv[:, :d]), tl(rkv[:, d:2 * d]), tl(rkv[:, 2 * d:])
    d0_l, d1_l = tl(dec[:, :d]), tl(dec[:, d:])
    a0_l, a1_l = tl(icl[:, :d]), tl(icl[:, d:])
    hc = lambda t: _head_const(t.reshape(h_heads, n), b)
    tb = _pick(math.gcd(rows.n_ctx, rows.n_lat), (32, 16, 8))
    yf, yr = _rwkv_scan(r_l, k_l, v_l, d0_l, d1_l, a0_l, a1_l, hc(k_k), hc(k_a), rows.n_ctx, tb)
    o_l = _rwkv_readout(r_l, k_l, v_l, a0_l, a1_l, yf, yr, hc(k_a), hc(r_k), hc(gn_w), hc(gn_b), tb)
    o = _from_lanes(o_l, b, lt, h_heads, n)

    og_specs = [pl.BlockSpec((tm, d), lambda i, j: (i, 0)), pl.BlockSpec((tm, d), lambda i, j: (i, 0))]
    return _proj_residual("rwkv_out", [o, gate], og_specs, lambda a, i, g_: (a[0][...] * a[1][...]).astype(BF16),
                          wo.astype(BF16), xa, mods, rows, 2, tm, tn)


def _mla_layer(xa, xl, mods, norm_g, rows_u, rows_l, p, tm_u, tm_l):
    wdown, qnorm, kvnorm, wuq, wukv, wo = p
    m, d = xa.shape
    b, lt, lc, ll = rows_u.batch, rows_u.per_batch, rows_u.n_ctx, rows_u.n_lat
    ql, kvl = qnorm.shape[0], kvnorm.shape[0]
    heads = wuq.shape[1] // (QK_NOPE + QK_ROPE)
    hq = 2 * LANES

    npairs = QK_ROPE // 4
    pos = jnp.arange(ll)
    inv_freq = jnp.float32(ROPE_THETA) ** (-jnp.arange(npairs, dtype=F32) / npairs)
    ang_r = (pos // GRID_W).astype(F32)[:, None] * inv_freq
    ang_c = (pos % GRID_W).astype(F32)[:, None] * inv_freq
    cos64 = jnp.concatenate([jnp.cos(ang_r)] * 2 + [jnp.cos(ang_c)] * 2, axis=1)
    sin64 = jnp.concatenate([-jnp.sin(ang_r), jnp.sin(ang_r), -jnp.sin(ang_c), jnp.sin(ang_c)], axis=1)
    padl = ((lc, 0), (0, LANES - QK_ROPE))
    cos_t = jnp.tile(jnp.pad(cos64, padl, constant_values=1.0), (b, 1))
    sin_t = jnp.tile(jnp.pad(sin64, padl), (b, 1))

    h1 = _norm_mod(xa, norm_g, mods, rows_u, 0, 1, BF16)

    nd = _round_up(ql + kvl + QK_ROPE, LANES)
    wd = jnp.pad(wdown, ((0, 0), (0, nd - wdown.shape[1]))).astype(BF16)

    def down_epi(acc, e, o, i, j):
        qn_ref, kvn_ref, cos_ref, sin_ref = e
        o[0][...] = (_rms(acc[:, :ql]) * qn_ref[...]).astype(BF16)
        o[1][...] = (_rms(acc[:, ql:ql + kvl]) * kvn_ref[...]).astype(BF16)
        o[2][...] = _rope(acc[:, ql + kvl:], cos_ref[...], sin_ref[...]).astype(BF16)

    tab_spec = pl.BlockSpec((tm_u, LANES), lambda i, j: (i, 0))
    cq, ckv, krope = _mm(
        "mla_down", [h1], [pl.BlockSpec((tm_u, d), lambda i, j: (i, 0))], _first_input, wd,
        [qnorm.reshape(1, ql), kvnorm.reshape(1, kvl), cos_t, sin_t],
        [pl.BlockSpec((1, ql), lambda i, j: (0, 0)), pl.BlockSpec((1, kvl), lambda i, j: (0, 0)), tab_spec, tab_spec],
        down_epi,
        [jax.ShapeDtypeStruct((m, ql), BF16), jax.ShapeDtypeStruct((m, kvl), BF16),
         jax.ShapeDtypeStruct((m, LANES), BF16)],
        [pl.BlockSpec((tm_u, ql), lambda i, j: (i, 0)), pl.BlockSpec((tm_u, kvl), lambda i, j: (i, 0)),
         pl.BlockSpec((tm_u, LANES), lambda i, j: (i, 0))],
        m, d, nd, tm_u, nd)

    wq = wuq.reshape(ql, heads, QK_NOPE + QK_ROPE)
    wq = jnp.pad(wq, ((0, 0), (0, 0), (0, hq - QK_NOPE - QK_ROPE))).reshape(ql, heads * hq).astype(BF16)
    hpt = _pick(heads, (4, 2, 1))

    def q_epi(acc, e, o, i, j):
        cos_ref, sin_ref = e
        for hh in range(hpt):
            lo = hh * hq
            o[0][:, lo:lo + QK_NOPE] = acc[:, lo:lo + QK_NOPE].astype(BF16)
            o[0][:, lo + QK_NOPE:lo + hq] = _rope(acc[:, lo + QK_NOPE:lo + hq], cos_ref[...], sin_ref[...]).astype(BF16)

    q = _mm("mla_q_up", [cq], [pl.BlockSpec((tm_u, ql), lambda i, j: (i, 0))], _first_input, wq,
            [cos_t, sin_t], [tab_spec, tab_spec], q_epi,
            [jax.ShapeDtypeStruct((m, heads * hq), BF16)], [pl.BlockSpec((tm_u, hpt * hq), lambda i, j: (i, j))],
            m, ql, heads * hq, tm_u, hpt * hq)[0]

    wkv = wukv.reshape(kvl, heads, QK_NOPE + V_HEAD)
    wk = wkv[:, :, :QK_NOPE].reshape(kvl, heads * QK_NOPE).astype(BF16)
    wv = wkv[:, :, QK_NOPE:].reshape(kvl, heads * V_HEAD).astype(BF16)
    kcat, vals = pl.pallas_call(
        functools.partial(_kv_up_kernel, hpt=hpt),
        grid=(m // tm_u, heads // hpt),
        in_specs=[pl.BlockSpec((tm_u, kvl), lambda i, j: (i, 0)),
                  pl.BlockSpec((kvl, hpt * QK_NOPE), lambda i, j: (0, j)),
                  pl.BlockSpec((kvl, hpt * V_HEAD), lambda i, j: (0, j)),
                  pl.BlockSpec((tm_u, LANES), lambda i, j: (i, 0))],
        out_specs=[pl.BlockSpec((tm_u, hpt * hq), lambda i, j: (i, j)),
                   pl.BlockSpec((tm_u, hpt * V_HEAD), lambda i, j: (i, j))],
        out_shape=[jax.ShapeDtypeStruct((m, heads * hq), BF16), jax.ShapeDtypeStruct((m, heads * V_HEAD), BF16)],
        compiler_params=_cparams(2),
        name="mla_kv_up",
    )(ckv, wk, wv, krope)

    tq = _pick(math.gcd(lc, ll), (256, 128, 64))
    scale = (QK_NOPE + QK_ROPE) ** -0.5
    o = pl.pallas_call(
        functools.partial(_attn_kernel, scale=scale),
        grid=(b, heads, ll // tq),
        in_specs=[pl.BlockSpec((None, tq, hq), lambda bb, hh, i: (bb, lc // tq + i, hh)),
                  pl.BlockSpec((None, lt, hq), lambda bb, hh, i: (bb, 0, hh)),
                  pl.BlockSpec((None, lt, V_HEAD), lambda bb, hh, i: (bb, 0, hh))],
        out_specs=pl.BlockSpec((None, tq, V_HEAD), lambda bb, hh, i: (bb, i, hh)),
        out_shape=jax.ShapeDtypeStruct((b, ll, heads * V_HEAD), BF16),
        compiler_params=_cparams(3),
        name="mla_attn",
    )(q.reshape(b, lt, heads * hq), kcat.reshape(b, lt, heads * hq), vals.reshape(b, lt, heads * V_HEAD))

    ko = heads * V_HEAD
    return _proj_residual("mla_out", [o.reshape(b * ll, ko)], [pl.BlockSpec((tm_l, ko), lambda i, j: (i, 0))],
                          _first_input, wo.astype(BF16), xl, mods, rows_l, 2,
                          tm_l, _pick(d, (512, 256, 128)))


def kernel(x, c, ctx, c_ctx, ada_w, ada_b, norm_g, final_g, rk_mu, rk_wr, rk_wk, rk_wv, rk_wo, rk_w0, rk_w1, rk_w2, rk_a0, rk_a1, rk_a2, rk_g1, rk_g2, rk_kk, rk_ka, rk_rk, rk_gn_w, rk_gn_b, ml_wdown, ml_qnorm, ml_kvnorm, ml_wuq, ml_wukv, ml_wo, ff_wup, ff_conv, ff_convb, ff_wdown):
    b, ll, d = x.shape
    lc = ctx.shape[1]
    depth = ada_w.shape[0]
    heads_r = rk_rk.shape[1]
    assert depth == 2 and rk_mu.shape[0] == 1 and ml_wdown.shape[0] == 1, "one RWKV layer then one MLA layer"
    assert b + 1 <= MOD_ROWS and b * heads_r <= LANES
    sub = _pick(math.gcd(lc, ll), (256, 128, 64, 32, 16, 8))
    rows_u = _Rows(b, lc, ll, sub)
    rows_l = _Rows(b, 0, ll, sub)
    tm_u = _pick(rows_u.total, tuple(t for t in (1024, 512, 256, 128, 64) if t % sub == 0))
    tm_l = _pick(rows_l.total, tuple(t for t in (1024, 512, 256, 128, 64) if t % sub == 0))

    cvec = jnp.zeros((MOD_ROWS, d), F32).at[:b].set(c).at[b].set(c_ctx)
    mods = _ada(cvec, ada_w, ada_b)

    xa = jnp.concatenate([ctx, x], axis=1).reshape(rows_u.total, d)

    p0 = (rk_mu[0], rk_wr[0], rk_wk[0], rk_wv[0], rk_wo[0], rk_w0[0], rk_w1[0], rk_w2[0], rk_a0[0], rk_a1[0],
          rk_a2[0], rk_g1[0], rk_g2[0], rk_kk[0], rk_ka[0], rk_rk[0], rk_gn_w[0], rk_gn_b[0])
    xa = _rwkv_layer(xa, mods[0], norm_g[0, 0], rows_u, p0, tm_u)
    xa = _conv_ffn(xa, norm_g[0, 1], mods[0], rows_u, ff_wup[0], ff_conv[0], ff_convb[0], ff_wdown[0], tm_u)

    xl = xa.reshape(b, rows_u.per_batch, d)[:, lc:].reshape(rows_l.total, d)
    p1 = (ml_wdown[0], ml_qnorm[0], ml_kvnorm[0], ml_wuq[0], ml_wukv[0], ml_wo[0])
    xl = _mla_layer(xa, xl, mods[1], norm_g[1, 0], rows_u, rows_l, p1, tm_u, tm_l)
    xl = _conv_ffn(xl, norm_g[1, 1], mods[1], rows_l, ff_wup[1], ff_conv[1], ff_convb[1], ff_wdown[1], tm_l)
    return _final_norm(xl, final_g).reshape(b, ll, d)
```

```python
import functools
import math

import jax
import jax.numpy as jnp
from jax import lax
from jax.experimental import pallas as pl
from jax.experimental.pallas import tpu as pltpu

F32 = jnp.float32
BF16 = jnp.bfloat16

RMS_EPS = 1e-6
GN_EPS = 64e-5
QK_NOPE = 128
QK_ROPE = 64
V_HEAD = 128
GRID_W = 64
ROPE_THETA = 10000.0
LANES = 128
SUBLANES = 8
BF16_ROWS = 16
PITCH = 72
VMEM_LIMIT = 56 * 1024 * 1024
MOD_ROWS = 8


def _cparams(n_axes):
    return pltpu.CompilerParams(dimension_semantics=("arbitrary",) * n_axes, vmem_limit_bytes=VMEM_LIMIT)


def _pick(m, cands):
    for c in cands:
        if m % c == 0:
            return c
    raise ValueError(f"no tile for {m} in {cands}")


def _round_up(x, m):
    return (x + m - 1) // m * m


class _Rows:
    def __init__(self, batch, n_ctx, n_lat, sub):
        self.batch, self.n_ctx, self.n_lat, self.sub = batch, n_ctx, n_lat, sub
        self.per_batch = n_ctx + n_lat
        self.total = batch * self.per_batch
        self.spb = self.per_batch // sub
        self.ctx_sub = n_ctx // sub

    def info(self, q):
        b = q // self.spb
        w = q - b * self.spb
        if self.ctx_sub:
            idx = jnp.where(w < self.ctx_sub, self.batch, b)
            start = (w == 0) | (w == self.ctx_sub)
            end = (w == self.ctx_sub - 1) | (w == self.spb - 1)
        else:
            idx = b
            start = w == 0
            end = w == self.spb - 1
        return idx, start, end


def _silu(x):
    return x * jax.nn.sigmoid(x)


def _rms(x):
    return x * lax.rsqrt(jnp.mean(x * x, axis=-1, keepdims=True) + RMS_EPS)


def _ada_kernel(c_ref, w_ref, b_ref, o_ref):
    o_ref[...] = jnp.dot(_silu(c_ref[...]), w_ref[...], preferred_element_type=F32) + b_ref[...]


def _ada(cvec, ada_w, ada_b):
    depth, d, n = ada_w.shape
    tn = _pick(n, (512, 256, 128))
    return pl.pallas_call(
        _ada_kernel,
        grid=(depth, n // tn),
        in_specs=[pl.BlockSpec((MOD_ROWS, d), lambda l, j: (0, 0)),
                  pl.BlockSpec((None, d, tn), lambda l, j: (l, 0, j)),
                  pl.BlockSpec((None, 1, tn), lambda l, j: (l, 0, j))],
        out_specs=pl.BlockSpec((None, MOD_ROWS, tn), lambda l, j: (l, 0, j)),
        out_shape=jax.ShapeDtypeStruct((depth, MOD_ROWS, n), F32),
        compiler_params=_cparams(2),
        name="ada_mod",
    )(cvec, ada_w, ada_b.reshape(depth, 1, n))


def _norm_mod_kernel(x_ref, g_ref, m_ref, o_ref, *, rows, tm, d, sh_col, sc_col):
    i = pl.program_id(0)
    nsub = tm // rows.sub
    for s in range(nsub):
        idx, _, _ = rows.info(i * nsub + s)
        sl = slice(s * rows.sub, (s + 1) * rows.sub)
        sh = m_ref[pl.ds(idx, 1), sh_col * d:(sh_col + 1) * d]
        sc = m_ref[pl.ds(idx, 1), sc_col * d:(sc_col + 1) * d]
        y = _rms(x_ref[sl, :]) * g_ref[...]
        o_ref[sl, :] = (y * (1.0 + sc) + sh).astype(o_ref.dtype)


def _norm_mod(x2, g, mods, rows, sh_col, sc_col, out_dtype):
    m, d = x2.shape
    tm = _pick(m, tuple(t for t in (512, 256, 128, 64, 32, 16, 8) if t % rows.sub == 0))
    return pl.pallas_call(
        functools.partial(_norm_mod_kernel, rows=rows, tm=tm, d=d, sh_col=sh_col, sc_col=sc_col),
        grid=(m // tm,),
        in_specs=[pl.BlockSpec((tm, d), lambda i: (i, 0)),
                  pl.BlockSpec((1, d), lambda i: (0, 0)),
                  pl.BlockSpec(mods.shape, lambda i: (0, 0))],
        out_specs=pl.BlockSpec((tm, d), lambda i: (i, 0)),
        out_shape=jax.ShapeDtypeStruct((m, d), out_dtype),
        compiler_params=_cparams(1),
        name="norm_mod",
    )(x2, g.reshape(1, d), mods)


def _final_norm_kernel(x_ref, g_ref, o_ref):
    o_ref[...] = _rms(x_ref[...]) * g_ref[...]


def _final_norm(x2, g):
    m, d = x2.shape
    tm = _pick(m, (512, 256, 128, 64))
    return pl.pallas_call(
        _final_norm_kernel,
        grid=(m // tm,),
        in_specs=[pl.BlockSpec((tm, d), lambda i: (i, 0)), pl.BlockSpec((1, d), lambda i: (0, 0))],
        out_specs=pl.BlockSpec((tm, d), lambda i: (i, 0)),
        out_shape=jax.ShapeDtypeStruct((m, d), F32),
        compiler_params=_cparams(1),
        name="final_norm",
    )(x2, g.reshape(1, d))


def _edge_masks(rows, i, tm):
    nsub = tm // rows.sub
    rid = lax.broadcasted_iota(jnp.int32, (tm, 1), 0)
    is_start = jnp.zeros((tm, 1), jnp.bool_)
    is_end = jnp.zeros((tm, 1), jnp.bool_)
    for s in range(nsub):
        _, st, en = rows.info(i * nsub + s)
        is_start = is_start | (rid == jnp.where(st, s * rows.sub, -1))
        is_end = is_end | (rid == jnp.where(en, (s + 1) * rows.sub - 1, -1))
    return rid, is_start, is_end


def _shifted(cur, prev_row, next_row, rid, is_start, is_end, tm):
    prev = jnp.where(rid == 0, prev_row, pltpu.roll(cur, 1, 0))
    nxt = jnp.where(rid == tm - 1, next_row, pltpu.roll(cur, tm - 1, 0))
    return jnp.where(is_start, 0.0, prev), jnp.where(is_end, 0.0, nxt)


def _rwkv_prep_kernel(x_ref, xp_ref, xn_ref, g_ref, m_ref, h_ref, xx_ref, *, rows, tm, d):
    i = pl.program_id(0)
    nsub = tm // rows.sub
    total_sub = rows.total // rows.sub

    def mod_rows(q):
        idx, _, _ = rows.info(q)
        return m_ref[pl.ds(idx, 1), 0:d], m_ref[pl.ds(idx, 1), d:2 * d]

    def h_of(x, q):
        sh, sc = mod_rows(q)
        return (_rms(x) * g_ref[...]) * (1.0 + sc) + sh

    for s in range(nsub):
        sl = slice(s * rows.sub, (s + 1) * rows.sub)
        h_ref[sl, :] = h_of(x_ref[sl, :], i * nsub + s)
    h = h_ref[...]
    hp = h_of(xp_ref[SUBLANES - 1:SUBLANES, :], jnp.maximum(i * nsub - 1, 0))
    hn = h_of(xn_ref[0:1, :], jnp.minimum((i + 1) * nsub, total_sub - 1))
    rid, is_start, is_end = _edge_masks(rows, i, tm)
    prev, nxt = _shifted(h, hp, hn, rid, is_start, is_end, tm)
    xx_ref[...] = 0.5 * (prev + nxt) - h


def _rwkv_prep(x2, g, mods, rows):
    m, d = x2.shape
    tm = max(_pick(m, (256, 128, 64)), rows.sub)
    r8 = tm // SUBLANES
    nblk8 = m // SUBLANES
    return pl.pallas_call(
        functools.partial(_rwkv_prep_kernel, rows=rows, tm=tm, d=d),
        grid=(m // tm,),
        in_specs=[pl.BlockSpec((tm, d), lambda i: (i, 0)),
                  pl.BlockSpec((SUBLANES, d), lambda i: (jnp.maximum(i * r8 - 1, 0), 0)),
                  pl.BlockSpec((SUBLANES, d), lambda i: (jnp.minimum((i + 1) * r8, nblk8 - 1), 0)),
                  pl.BlockSpec((1, d), lambda i: (0, 0)),
                  pl.BlockSpec(mods.shape, lambda i: (0, 0))],
        out_specs=[pl.BlockSpec((tm, d), lambda i: (i, 0)), pl.BlockSpec((tm, d), lambda i: (i, 0))],
        out_shape=[jax.ShapeDtypeStruct((m, d), F32), jax.ShapeDtypeStruct((m, d), F32)],
        compiler_params=_cparams(1),
        name="rwkv_prep",
    )(x2, x2, x2, g.reshape(1, d), mods)


def _mm_kernel(*refs, na, ne, no, prologue, epilogue, period):
    a_refs = refs[:na]
    w_ref = refs[na]
    e_refs = refs[na + 1:na + 1 + ne]
    o_refs = refs[na + 1 + ne:na + 1 + ne + no]
    a_scr = refs[-1]
    i = pl.program_id(0)
    j = pl.program_id(1)

    @pl.when(j % period == 0)
    def _():
        a_scr[...] = prologue(a_refs, i, j // period)

    acc = jnp.dot(a_scr[...], w_ref[...], preferred_element_type=F32)
    epilogue(acc, e_refs, o_refs, i, j)


def _mm(name, a_arrays, a_specs, prologue, w, e_arrays, e_specs, epilogue, out_shapes, out_specs,
        m, k, n, tm, tn, period=None):
    nj = n // tn
    period = nj if period is None else period
    return pl.pallas_call(
        functools.partial(_mm_kernel, na=len(a_arrays), ne=len(e_arrays), no=len(out_shapes),
                          prologue=prologue, epilogue=epilogue, period=period),
        grid=(m // tm, nj),
        in_specs=list(a_specs) + [pl.BlockSpec((k, tn), lambda i, j: (0, j))] + list(e_specs),
        out_specs=list(out_specs),
        out_shape=list(out_shapes),
        scratch_shapes=[pltpu.VMEM((tm, k), BF16)],
        compiler_params=_cparams(2),
        name=name,
    )(*a_arrays, w, *e_arrays)


def _gated_residual_epilogue(rows, tm):
    nsub = tm // rows.sub

    def epi(acc, e_refs, o_refs, i, j):
        res_ref, m_ref = e_refs
        for s in range(nsub):
            idx, _, _ = rows.info(i * nsub + s)
            sl = slice(s * rows.sub, (s + 1) * rows.sub)
            o_refs[0][sl, :] = res_ref[sl, :] + m_ref[pl.ds(idx, 1), :] * acc[sl, :]
    return epi


def _store_epilogue(acc, e_refs, o_refs, i, j):
    o_refs[0][...] = acc.astype(o_refs[0].dtype)


def _first_input(a_refs, i, grp):
    return a_refs[0][...]


def _proj_residual(name, a_arrays, a_specs, prologue, w, res, mods, rows, gate_col, tm, tn):
    m, d = res.shape
    k = w.shape[0]
    gate_blk = gate_col * (d // tn)
    return _mm(name, a_arrays, a_specs, prologue, w,
               [res, mods],
               [pl.BlockSpec((tm, tn), lambda i, j: (i, j)),
                pl.BlockSpec((MOD_ROWS, tn), lambda i, j: (0, gate_blk + j))],
               _gated_residual_epilogue(rows, tm),
               [jax.ShapeDtypeStruct((m, d), F32)],
               [pl.BlockSpec((tm, tn), lambda i, j: (i, j))],
               m, k, d, tm, tn)[0]


def _softplus(z):
    return jnp.maximum(z, 0.0) + jnp.log(1.0 + jnp.exp(-jnp.abs(z)))


def _pack_kernel(x_ref, o_ref, z_ref, *, b, h, n, tiles):
    hp = LANES // n
    if b * h < LANES:
        z_ref[...] = jnp.zeros_like(z_ref)
    for bb in range(b):
        for c in range(h // hp):
            t = x_ref[bb, :, c * LANES:(c + 1) * LANES].T
            for hh in range(hp):
                lane = bb * h + c * hp + hh
                z_ref[lane * PITCH:lane * PITCH + n, :] = t[hh * n:(hh + 1) * n, :]
    for k in range(n):
        m = z_ref[pl.ds(k, LANES, stride=PITCH), :].T
        if tiles:
            o_ref[k // SUBLANES, :, k % SUBLANES, :] = m
        else:
            o_ref[k] = m


def _pack(x2, b, lt, h, n, g0, groups, tiles):
    d = h * n
    tt = LANES
    x3 = x2.reshape(b, lt, x2.shape[1])
    if tiles:
        oshape, oblk = (groups, n // SUBLANES, lt, SUBLANES, LANES), (None, n // SUBLANES, tt, SUBLANES, LANES)
        omap = lambda g, ti: (g, 0, ti, 0, 0)
    else:
        oshape, oblk = (groups, n, lt, LANES), (None, n, tt, LANES)
        omap = lambda g, ti: (g, 0, ti, 0)
    return pl.pallas_call(
        functools.partial(_pack_kernel, b=b, h=h, n=n, tiles=tiles),
        grid=(groups, lt // tt),
        in_specs=[pl.BlockSpec((b, tt, d), lambda g, ti: (0, ti, g0 + g))],
        out_specs=pl.BlockSpec(oblk, omap),
        out_shape=jax.ShapeDtypeStruct(oshape, F32),
        scratch_shapes=[pltpu.VMEM((LANES * PITCH, tt), F32)],
        compiler_params=_cparams(2),
        name="lane_pack",
    )(x3)


def _unpack_kernel(o_ref, x_ref, z_ref, *, b, h, n):
    hp = LANES // n
    for k in range(n):
        z_ref[pl.ds(k, LANES, stride=PITCH), :] = o_ref[k // SUBLANES, :, k % SUBLANES, :].T
    for bb in range(b):
        for c in range(h // hp):
            parts = [z_ref[(bb * h + c * hp + hh) * PITCH:(bb * h + c * hp + hh) * PITCH + n, :] for hh in range(hp)]
            x_ref[bb, :, c * LANES:(c + 1) * LANES] = jnp.concatenate(parts, axis=0).T


def _unpack(o5, b, lt, h, n):
    d = h * n
    tt = LANES
    return pl.pallas_call(
        functools.partial(_unpack_kernel, b=b, h=h, n=n),
        grid=(lt // tt,),
        in_specs=[pl.BlockSpec((n // SUBLANES, tt, SUBLANES, LANES), lambda ti: (0, ti, 0, 0))],
        out_specs=pl.BlockSpec((b, tt, d), lambda ti: (0, ti, 0)),
        out_shape=jax.ShapeDtypeStruct((b, lt, d), F32),
        scratch_shapes=[pltpu.VMEM((LANES * PITCH, tt), F32)],
        compiler_params=_cparams(1),
        name="lane_unpack",
    )(o5).reshape(b * lt, d)


def _scan_kernel(rkf, vf, df, af, rkr, vr, dr, ar, kk_ref, ka_ref, yf_ref, yr_ref, s_ref, vec_ref,
                 *, tb, n, vchunk):
    j = pl.program_id(0)

    @pl.when(j == 0)
    def _():
        s_ref[...] = jnp.zeros_like(s_ref)

    kkc = kk_ref[...][:, None, :]
    kac = ka_ref[...][:, None, :]
    nvg = n // SUBLANES
    streams = ((rkf, vf, df, af, yf_ref), (rkr, vr, dr, ar, yr_ref))

    for dirn, (rk_, _, _, a_, _) in enumerate(streams):
        kt = rk_[1]
        at = a_[...]
        kx = kt * kkc
        nrm = jnp.sqrt(jnp.sum(kx * kx, axis=0, keepdims=True))
        kkn = kx / jnp.maximum(nrm, 1e-12)
        vec_ref[dirn, 0] = -kkn
        vec_ref[dirn, 1] = kkn * at
        vec_ref[dirn, 2] = kt * (1.0 + (at - 1.0) * kac)

    def bc(row):
        return jnp.broadcast_to(row, (SUBLANES, LANES))

    def tile(vg):
        return slice(vg * SUBLANES, (vg + 1) * SUBLANES)

    def make_step(dirn, rk_, v_, d_, y_):
        def step(i, sa):
            tt = i if dirn == 0 else tb - 1 - i
            tnext = jnp.minimum(tt + 1, tb - 1) if dirn == 0 else jnp.maximum(tt - 1, 0)
            sa_next = [None] * nvg
            for c in range(nvg // vchunk):
                vgs = [c * vchunk + q for q in range(vchunk)]
                vv = [v_[vg, tt] for vg in vgs]
                yy = [jnp.zeros((SUBLANES, LANES), F32) for _ in vgs]
                sn = [jnp.zeros((SUBLANES, LANES), F32) for _ in vgs]
                for kk in range(n):
                    wb = bc(d_[kk, pl.ds(tt, 1), :])
                    bb = bc(vec_ref[dirn, 1, kk, pl.ds(tt, 1), :])
                    kb = bc(vec_ref[dirn, 2, kk, pl.ds(tt, 1), :])
                    rb = bc(rk_[0, kk, pl.ds(tt, 1), :])
                    an = bc(vec_ref[dirn, 0, kk, pl.ds(tnext, 1), :])
                    for q, vg in enumerate(vgs):
                        snew = s_ref[dirn, kk, tile(vg), :] * wb + sa[vg] * bb + vv[q] * kb
                        s_ref[dirn, kk, tile(vg), :] = snew
                        yy[q] = yy[q] + snew * rb
                        sn[q] = sn[q] + snew * an
                for q, vg in enumerate(vgs):
                    y_[vg, tt] = yy[q]
                    sa_next[vg] = sn[q]
            return tuple(sa_next)
        return step

    for dirn, (rk_, v_, d_, _, y_) in enumerate(streams):
        t0 = 0 if dirn == 0 else tb - 1
        sa0 = [jnp.zeros((SUBLANES, LANES), F32) for _ in range(nvg)]
        for kk in range(n):
            a0 = bc(vec_ref[dirn, 0, kk, t0:t0 + 1, :])
            for vg in range(nvg):
                sa0[vg] = sa0[vg] + s_ref[dirn, kk, tile(vg), :] * a0
        lax.fori_loop(0, tb, make_step(dirn, rk_, v_, d_, y_), tuple(sa0))


def _rwkv_scan(rk, v, dec, icl, kk_c, ka_c, n_ctx, tb):
    _, n, lt, ln = rk.shape
    nvg = n // SUBLANES
    nb = lt // tb
    nc = n_ctx // tb

    def rev(j):
        return jnp.where(j < nc, nc - 1 - j, nb - 1 - (j - nc))

    def specs(tmap, dirn):
        return [pl.BlockSpec((2, n, tb, ln), lambda j: (0, 0, tmap(j), 0)),
                pl.BlockSpec((None, nvg, tb, SUBLANES, ln), lambda j: (0, 0, tmap(j), 0, 0)),
                pl.BlockSpec((None, n, tb, ln), lambda j: (dirn, 0, tmap(j), 0)),
                pl.BlockSpec((None, n, tb, ln), lambda j: (dirn, 0, tmap(j), 0))]

    ident = lambda j: j
    const = pl.BlockSpec((n, ln), lambda j: (0, 0))
    yshape = jax.ShapeDtypeStruct((nvg, lt, SUBLANES, ln), F32)
    return pl.pallas_call(
        functools.partial(_scan_kernel, tb=tb, n=n, vchunk=min(8, nvg)),
        grid=(nb,),
        in_specs=specs(ident, 0) + specs(rev, 1) + [const, const],
        out_specs=[pl.BlockSpec((nvg, tb, SUBLANES, ln), lambda j: (0, j, 0, 0)),
                   pl.BlockSpec((nvg, tb, SUBLANES, ln), lambda j: (0, rev(j), 0, 0))],
        out_shape=[yshape, yshape],
        scratch_shapes=[pltpu.VMEM((2, n, n, ln), F32), pltpu.VMEM((2, 3, n, tb, ln), F32)],
        compiler_params=_cparams(1),
        name="rwkv_scan",
    )(rk, v, dec, icl, rk, v, dec, icl, kk_c, ka_c)


def _readout_kernel(rk_ref, v_ref, a_ref, yf_ref, yr_ref, ka_ref, rkc_ref, gw_ref, gb_ref, o_ref, *, n):
    y = yf_ref[...] + yr_ref[...]
    mean = jnp.sum(jnp.sum(y, axis=0, keepdims=True), axis=2, keepdims=True) * (1.0 / n)
    yc = y - mean
    var = jnp.sum(jnp.sum(yc * yc, axis=0, keepdims=True), axis=2, keepdims=True) * (1.0 / n)
    yn = (yc * lax.rsqrt(var + GN_EPS)) * gw_ref[...][:, None] + gb_ref[...][:, None]
    r = rk_ref[0]
    k = rk_ref[1]
    kac = ka_ref[...][:, None, :]
    rrk = r * rkc_ref[...][:, None, :]
    kd0 = k * (1.0 + (a_ref[0] - 1.0) * kac)
    kd1 = k * (1.0 + (a_ref[1] - 1.0) * kac)
    bonus = jnp.sum(rrk * kd0, axis=0) + jnp.sum(rrk * kd1, axis=0)
    o_ref[...] = yn + bonus[None, :, None, :] * v_ref[...]


def _rwkv_readout(rk, v, icl, yf, yr, ka_c, rk_c, gw_t, gb_t, tb):
    _, n, lt, ln = rk.shape
    nvg = n // SUBLANES
    tile = pl.BlockSpec((nvg, tb, SUBLANES, ln), lambda j: (0, j, 0, 0))
    chan2 = pl.BlockSpec((2, n, tb, ln), lambda j: (0, 0, j, 0))
    const = pl.BlockSpec((n, ln), lambda j: (0, 0))
    const_t = pl.BlockSpec((nvg, SUBLANES, ln), lambda j: (0, 0, 0))
    return pl.pallas_call(
        functools.partial(_readout_kernel, n=n),
        grid=(lt // tb,),
        in_specs=[chan2, pl.BlockSpec((None, nvg, tb, SUBLANES, ln), lambda j: (0, 0, j, 0, 0)), chan2, tile, tile,
                  const, const, const_t, const_t],
        out_specs=tile,
        out_shape=jax.ShapeDtypeStruct((nvg, lt, SUBLANES, ln), F32),
        compiler_params=_cparams(1),
        name="rwkv_readout",
    )(rk, v, icl, yf, yr, ka_c, rk_c, gw_t, gb_t)


def _ffn_up_kernel(h_ref, hp_ref, hn_ref, wg_ref, wv_ref, cw_ref, cb_ref, o_ref, *, rows, tm):
    i = pl.program_id(0)
    h = h_ref[...]
    wg = wg_ref[...]
    gate = jnp.dot(h, wg, preferred_element_type=F32)
    val = jnp.dot(h, wv_ref[...], preferred_element_type=F32)
    gp = jnp.dot(hp_ref[...], wg, preferred_element_type=F32)[BF16_ROWS - 1:BF16_ROWS, :]
    gn = jnp.dot(hn_ref[...], wg, preferred_element_type=F32)[0:1, :]
    rid, is_start, is_end = _edge_masks(rows, i, tm)
    prev, nxt = _shifted(gate, gp, gn, rid, is_start, is_end, tm)
    conv = prev * cw_ref[0:1, :] + gate * cw_ref[1:2, :] + nxt * cw_ref[2:3, :] + cb_ref[...]
    o_ref[...] = (_silu(conv) * val).astype(o_ref.dtype)


def _ffn_up(hf, wup, conv_w, conv_b, rows, tm, tn):
    m, d = hf.shape
    f = wup.shape[1] // 2
    r8 = tm // BF16_ROWS
    nblk8 = m // BF16_ROWS
    njf = f // tn
    cw = jnp.zeros((SUBLANES, f), F32).at[:conv_w.shape[0]].set(conv_w)
    return pl.pallas_call(
        functools.partial(_ffn_up_kernel, rows=rows, tm=tm),
        grid=(m // tm, njf),
        in_specs=[pl.BlockSpec((tm, d), lambda i, j: (i, 0)),
                  pl.BlockSpec((BF16_ROWS, d), lambda i, j: (jnp.maximum(i * r8 - 1, 0), 0)),
                  pl.BlockSpec((BF16_ROWS, d), lambda i, j: (jnp.minimum((i + 1) * r8, nblk8 - 1), 0)),
                  pl.BlockSpec((d, tn), lambda i, j: (0, j)),
                  pl.BlockSpec((d, tn), lambda i, j: (0, j + njf)),
                  pl.BlockSpec((SUBLANES, tn), lambda i, j: (0, j)),
                  pl.BlockSpec((1, tn), lambda i, j: (0, j))],
        out_specs=pl.BlockSpec((tm, tn), lambda i, j: (i, j)),
        out_shape=jax.ShapeDtypeStruct((m, f), BF16),
        compiler_params=_cparams(2),
        name="ffn_up",
    )(hf, hf, hf, wup, wup, cw, conv_b.reshape(1, f))


def _conv_ffn(x2, g, mods, rows, wup, conv_w, conv_b, wdown, tm):
    m, d = x2.shape
    f = wdown.shape[0]
    hf = _norm_mod(x2, g, mods, rows, 3, 4, BF16)
    act = _ffn_up(hf, wup.astype(BF16), conv_w, conv_b, rows, tm, _pick(f, (512, 256, 128)))
    tmd = _pick(m, tuple(t for t in (512, 256, 128, 64) if t <= tm and t % rows.sub == 0))
    return _proj_residual("ffn_down", [act], [pl.BlockSpec((tmd, f), lambda i, j: (i, 0))],
                          _first_input, wdown.astype(BF16), x2, mods, rows, 5,
                          tmd, _pick(d, (512, 256, 128)))


def _rope(x, cos, sin_signed):
    lane = lax.broadcasted_iota(jnp.int32, x.shape, 1)
    first = (lane & 31) < 16
    swapped = jnp.where(first, pltpu.roll(x, LANES - 16, 1), pltpu.roll(x, 16, 1))
    return x * cos + swapped * sin_signed


def _kv_up_kernel(c_ref, wk_ref, wv_ref, kr_ref, ko_ref, vo_ref, *, hpt):
    c = c_ref[...]
    kn = jnp.dot(c, wk_ref[...], preferred_element_type=F32).astype(BF16)
    vo_ref[...] = jnp.dot(c, wv_ref[...], preferred_element_type=F32).astype(BF16)
    kr = kr_ref[...]
    for h in range(hpt):
        ko_ref[:, h * 2 * LANES:h * 2 * LANES + QK_NOPE] = kn[:, h * QK_NOPE:(h + 1) * QK_NOPE]
        ko_ref[:, h * 2 * LANES + QK_NOPE:(h + 1) * 2 * LANES] = kr


def _attn_kernel(q_ref, k_ref, v_ref, o_ref, *, hpa, hq):
    for h in range(hpa):
        q = q_ref[:, h * hq:(h + 1) * hq]
        k = k_ref[:, h * hq:(h + 1) * hq]
        s = lax.dot_general(q, k, (((1,), (1,)), ((), ())), preferred_element_type=F32)
        p = jnp.exp2(s - jnp.max(s, axis=-1, keepdims=True))
        denom = jnp.sum(p, axis=-1, keepdims=True)
        o = jnp.dot(p.astype(BF16), v_ref[:, h * V_HEAD:(h + 1) * V_HEAD], preferred_element_type=F32)
        o_ref[:, h * V_HEAD:(h + 1) * V_HEAD] = (o / denom).astype(o_ref.dtype)


def _head_const(p_hn, b):
    h, n = p_hn.shape
    y = jnp.broadcast_to(p_hn.T[:, None, :], (n, b, h)).reshape(n, b * h)
    return jnp.pad(y, ((0, 0), (0, LANES - b * h)))


def _rwkv_layer(xa, mods, norm_g, rows, p, tm):
    (mu, wr, wk, wv, wo, w0, w1, w2, a0, a1, a2, g1, g2, k_k, k_a, r_k, gn_w, gn_b) = p
    m, d = xa.shape
    h_heads, n = r_k.shape
    b, lt = rows.batch, rows.per_batch
    glora, dlora, alora = g1.shape[1], w1.shape[2], a1.shape[2]
    gw = _round_up(max(glora, 2 * dlora, 2 * alora), LANES)

    h, xx = _rwkv_prep(xa, norm_g, mods, rows)

    a_specs = [pl.BlockSpec((tm, d), lambda i, j: (i, 0)), pl.BlockSpec((tm, d), lambda i, j: (i, 0)),
               pl.BlockSpec((MOD_ROWS, d), lambda i, j: (0, 0))]

    def mix(a, i, grp):
        return (a[0][...] + a[1][...] * a[2][pl.ds(grp, 1), :]).astype(BF16)

    def pad_rows(t):
        return jnp.zeros((MOD_ROWS, d), F32).at[:t.shape[0]].set(t)

    tn = _pick(d, (512, 256, 128))
    w_rkv = jnp.concatenate([wr, wk, wv], axis=1).astype(BF16)
    rkv = _mm("rwkv_rkv", [h, xx, pad_rows(mu[jnp.array([0, 2, 3])])], a_specs, mix, w_rkv, [], [],
              _store_epilogue,
              [jax.ShapeDtypeStruct((m, 3 * d), F32)], [pl.BlockSpec((tm, tn), lambda i, j: (i, j))],
              m, d, 3 * d, tm, tn, period=d // tn)[0]

    def pad_cols(t):
        return jnp.pad(t, ((0, 0), (0, gw - t.shape[1])))

    w_lora = jnp.concatenate([pad_cols(g1), pad_cols(jnp.concatenate([w1[0], w1[1]], axis=1)),
                              pad_cols(jnp.concatenate([a1[0], a1[1]], axis=1))], axis=1).astype(BF16)

    def lora_act(acc, e, o, i, j):
        o[0][...] = jnp.where(j == 0, jax.nn.sigmoid(acc), jnp.where(j == 1, jnp.tanh(acc), acc)).astype(BF16)

    hid = _mm("rwkv_lora1", [h, xx, pad_rows(mu[jnp.array([5, 1, 4])])], a_specs, mix, w_lora, [], [], lora_act,
              [jax.ShapeDtypeStruct((m, 3 * gw), BF16)], [pl.BlockSpec((tm, gw), lambda i, j: (i, j))],
              m, d, 3 * gw, tm, gw, period=1)[0]

    def hid_spec(grp):
        return [pl.BlockSpec((tm, gw), lambda i, j: (i, grp))]

    ident = _first_input
    g2p = jnp.zeros((gw, d), F32).at[:glora].set(g2).astype(BF16)
    gate = _mm("rwkv_gate", [hid], hid_spec(0), ident, g2p, [], [],
               _store_epilogue,
               [jax.ShapeDtypeStruct((m, d), F32)], [pl.BlockSpec((tm, tn), lambda i, j: (i, j))],
               m, gw, d, tm, tn)[0]

    def two_dir(t2, lora):
        z = jnp.zeros((gw, 2 * d), F32)
        z = z.at[:lora, :d].set(t2[0]).at[lora:2 * lora, d:].set(t2[1])
        return z.astype(BF16)

    bias_spec = [pl.BlockSpec((1, tn), lambda i, j: (0, j))]

    def decay_epi(acc, e, o, i, j):
        w_log = -_softplus(-(e[0][...] + acc)) - 0.5
        o[0][...] = jnp.exp(-jnp.exp(w_log))

    dec = _mm("rwkv_decay", [hid], hid_spec(1), ident, two_dir(w2, dlora), [w0.reshape(1, 2 * d)], bias_spec,
              decay_epi, [jax.ShapeDtypeStruct((m, 2 * d), F32)], [pl.BlockSpec((tm, tn), lambda i, j: (i, j))],
              m, gw, 2 * d, tm, tn)[0]

    def iclr_epi(acc, e, o, i, j):
        o[0][...] = jax.nn.sigmoid(e[0][...] + acc)

    icl = _mm("rwkv_iclr", [hid], hid_spec(2), ident, two_dir(a2, alora), [a0.reshape(1, 2 * d)], bias_spec,
              iclr_epi, [jax.ShapeDtypeStruct((m, 2 * d), F32)], [pl.BlockSpec((tm, tn), lambda i, j: (i, j))],
              m, gw, 2 * d, tm, tn)[0]

    rk_l = _pack(rkv, b, lt, h_heads, n, 0, 2, False)
    v_l = _pack(rkv, b, lt, h_heads, n, 2, 1, True)
    dec_l = _pack(dec, b, lt, h_heads, n, 0, 2, False)
    icl_l = _pack(icl, b, lt, h_heads, n, 0, 2, False)
    hc = lambda t: _head_const(t.reshape(h_heads, n), b)
    ht = lambda t: hc(t).reshape(n // SUBLANES, SUBLANES, LANES)
    tb = _pick(math.gcd(rows.n_ctx, rows.n_lat), (32, 16, 8))
    yf, yr = _rwkv_scan(rk_l, v_l, dec_l, icl_l, hc(k_k), hc(k_a), rows.n_ctx, tb)
    o_l = _rwkv_readout(rk_l, v_l, icl_l, yf, yr, hc(k_a), hc(r_k), ht(gn_w), ht(gn_b), tb)
    o = _unpack(o_l, b, lt, h_heads, n)

    og_specs = [pl.BlockSpec((tm, d), lambda i, j: (i, 0)), pl.BlockSpec((tm, d), lambda i, j: (i, 0))]
    return _proj_residual("rwkv_out", [o, gate], og_specs, lambda a, i, g_: (a[0][...] * a[1][...]).astype(BF16),
                          wo.astype(BF16), xa, mods, rows, 2, tm, tn)


def _mla_layer(xa, xl, mods, norm_g, rows_u, rows_l, p, tm_u, tm_l):
    wdown, qnorm, kvnorm, wuq, wukv, wo = p
    m, d = xa.shape
    b, lt, lc, ll = rows_u.batch, rows_u.per_batch, rows_u.n_ctx, rows_u.n_lat
    ql, kvl = qnorm.shape[0], kvnorm.shape[0]
    heads = wuq.shape[1] // (QK_NOPE + QK_ROPE)
    hq = 2 * LANES

    npairs = QK_ROPE // 4
    pos = jnp.arange(ll)
    inv_freq = jnp.float32(ROPE_THETA) ** (-jnp.arange(npairs, dtype=F32) / npairs)
    ang_r = (pos // GRID_W).astype(F32)[:, None] * inv_freq
    ang_c = (pos % GRID_W).astype(F32)[:, None] * inv_freq
    cos64 = jnp.concatenate([jnp.cos(ang_r)] * 2 + [jnp.cos(ang_c)] * 2, axis=1)
    sin64 = jnp.concatenate([-jnp.sin(ang_r), jnp.sin(ang_r), -jnp.sin(ang_c), jnp.sin(ang_c)], axis=1)
    padl = ((lc, 0), (0, LANES - QK_ROPE))
    cos_t = jnp.tile(jnp.pad(cos64, padl, constant_values=1.0), (b, 1))
    sin_t = jnp.tile(jnp.pad(sin64, padl), (b, 1))

    h1 = _norm_mod(xa, norm_g, mods, rows_u, 0, 1, BF16)

    nd = _round_up(ql + kvl + QK_ROPE, LANES)
    wd = jnp.pad(wdown, ((0, 0), (0, nd - wdown.shape[1]))).astype(BF16)

    def down_epi(acc, e, o, i, j):
        qn_ref, kvn_ref, cos_ref, sin_ref = e
        o[0][...] = (_rms(acc[:, :ql]) * qn_ref[...]).astype(BF16)
        o[1][...] = (_rms(acc[:, ql:ql + kvl]) * kvn_ref[...]).astype(BF16)
        o[2][...] = _rope(acc[:, ql + kvl:], cos_ref[...], sin_ref[...]).astype(BF16)

    tab_spec = pl.BlockSpec((tm_u, LANES), lambda i, j: (i, 0))
    cq, ckv, krope = _mm(
        "mla_down", [h1], [pl.BlockSpec((tm_u, d), lambda i, j: (i, 0))], _first_input, wd,
        [qnorm.reshape(1, ql), kvnorm.reshape(1, kvl), cos_t, sin_t],
        [pl.BlockSpec((1, ql), lambda i, j: (0, 0)), pl.BlockSpec((1, kvl), lambda i, j: (0, 0)), tab_spec, tab_spec],
        down_epi,
        [jax.ShapeDtypeStruct((m, ql), BF16), jax.ShapeDtypeStruct((m, kvl), BF16),
         jax.ShapeDtypeStruct((m, LANES), BF16)],
        [pl.BlockSpec((tm_u, ql), lambda i, j: (i, 0)), pl.BlockSpec((tm_u, kvl), lambda i, j: (i, 0)),
         pl.BlockSpec((tm_u, LANES), lambda i, j: (i, 0))],
        m, d, nd, tm_u, nd)

    wq = wuq.reshape(ql, heads, QK_NOPE + QK_ROPE)
    wq = jnp.pad(wq, ((0, 0), (0, 0), (0, hq - QK_NOPE - QK_ROPE))).reshape(ql, heads * hq).astype(BF16)
    hpt = _pick(heads, (4, 2, 1))

    qscale = (QK_NOPE + QK_ROPE) ** -0.5 * math.log2(math.e)

    def q_epi(acc, e, o, i, j):
        cos_ref, sin_ref = e
        acc = acc * qscale
        for hh in range(hpt):
            lo = hh * hq
            o[0][:, lo:lo + QK_NOPE] = acc[:, lo:lo + QK_NOPE].astype(BF16)
            o[0][:, lo + QK_NOPE:lo + hq] = _rope(acc[:, lo + QK_NOPE:lo + hq], cos_ref[...], sin_ref[...]).astype(BF16)

    q = _mm("mla_q_up", [cq], [pl.BlockSpec((tm_u, ql), lambda i, j: (i, 0))], _first_input, wq,
            [cos_t, sin_t], [tab_spec, tab_spec], q_epi,
            [jax.ShapeDtypeStruct((m, heads * hq), BF16)], [pl.BlockSpec((tm_u, hpt * hq), lambda i, j: (i, j))],
            m, ql, heads * hq, tm_u, hpt * hq)[0]

    wkv = wukv.reshape(kvl, heads, QK_NOPE + V_HEAD)
    wk = wkv[:, :, :QK_NOPE].reshape(kvl, heads * QK_NOPE).astype(BF16)
    wv = wkv[:, :, QK_NOPE:].reshape(kvl, heads * V_HEAD).astype(BF16)
    kcat, vals = pl.pallas_call(
        functools.partial(_kv_up_kernel, hpt=hpt),
        grid=(m // tm_u, heads // hpt),
        in_specs=[pl.BlockSpec((tm_u, kvl), lambda i, j: (i, 0)),
                  pl.BlockSpec((kvl, hpt * QK_NOPE), lambda i, j: (0, j)),
                  pl.BlockSpec((kvl, hpt * V_HEAD), lambda i, j: (0, j)),
                  pl.BlockSpec((tm_u, LANES), lambda i, j: (i, 0))],
        out_specs=[pl.BlockSpec((tm_u, hpt * hq), lambda i, j: (i, j)),
                   pl.BlockSpec((tm_u, hpt * V_HEAD), lambda i, j: (i, j))],
        out_shape=[jax.ShapeDtypeStruct((m, heads * hq), BF16), jax.ShapeDtypeStruct((m, heads * V_HEAD), BF16)],
        compiler_params=_cparams(2),
        name="mla_kv_up",
    )(ckv, wk, wv, krope)

    tq = _pick(math.gcd(lc, ll), (256, 128, 64))
    hpa = _pick(heads, (2, 1))
    o = pl.pallas_call(
        functools.partial(_attn_kernel, hpa=hpa, hq=hq),
        grid=(b, heads // hpa, ll // tq),
        in_specs=[pl.BlockSpec((None, tq, hpa * hq), lambda bb, hh, i: (bb, lc // tq + i, hh)),
                  pl.BlockSpec((None, lt, hpa * hq), lambda bb, hh, i: (bb, 0, hh)),
                  pl.BlockSpec((None, lt, hpa * V_HEAD), lambda bb, hh, i: (bb, 0, hh))],
        out_specs=pl.BlockSpec((None, tq, hpa * V_HEAD), lambda bb, hh, i: (bb, i, hh)),
        out_shape=jax.ShapeDtypeStruct((b, ll, heads * V_HEAD), BF16),
        compiler_params=_cparams(3),
        name="mla_attn",
    )(q.reshape(b, lt, heads * hq), kcat.reshape(b, lt, heads * hq), vals.reshape(b, lt, heads * V_HEAD))

    ko = heads * V_HEAD
    return _proj_residual("mla_out", [o.reshape(b * ll, ko)], [pl.BlockSpec((tm_l, ko), lambda i, j: (i, 0))],
                          _first_input, wo.astype(BF16), xl, mods, rows_l, 2,
                          tm_l, _pick(d, (512, 256, 128)))


def kernel(x, c, ctx, c_ctx, ada_w, ada_b, norm_g, final_g, rk_mu, rk_wr, rk_wk, rk_wv, rk_wo, rk_w0, rk_w1, rk_w2, rk_a0, rk_a1, rk_a2, rk_g1, rk_g2, rk_kk, rk_ka, rk_rk, rk_gn_w, rk_gn_b, ml_wdown, ml_qnorm, ml_kvnorm, ml_wuq, ml_wukv, ml_wo, ff_wup, ff_conv, ff_convb, ff_wdown):
    b, ll, d = x.shape
    lc = ctx.shape[1]
    depth = ada_w.shape[0]
    heads_r = rk_rk.shape[1]
    assert depth == 2 and rk_mu.shape[0] == 1 and ml_wdown.shape[0] == 1, "one RWKV layer then one MLA layer"
    assert b + 1 <= MOD_ROWS and b * heads_r <= LANES
    sub = _pick(math.gcd(lc, ll), (256, 128, 64, 32, 16, 8))
    rows_u = _Rows(b, lc, ll, sub)
    rows_l = _Rows(b, 0, ll, sub)
    tm_u = _pick(rows_u.total, tuple(t for t in (1024, 512, 256, 128, 64) if t % sub == 0))
    tm_l = _pick(rows_l.total, tuple(t for t in (1024, 512, 256, 128, 64) if t % sub == 0))

    cvec = jnp.zeros((MOD_ROWS, d), F32).at[:b].set(c).at[b].set(c_ctx)
    mods = _ada(cvec, ada_w, ada_b)

    xa = jnp.concatenate([ctx, x], axis=1).reshape(rows_u.total, d)

    p0 = (rk_mu[0], rk_wr[0], rk_wk[0], rk_wv[0], rk_wo[0], rk_w0[0], rk_w1[0], rk_w2[0], rk_a0[0], rk_a1[0],
          rk_a2[0], rk_g1[0], rk_g2[0], rk_kk[0], rk_ka[0], rk_rk[0], rk_gn_w[0], rk_gn_b[0])
    xa = _rwkv_layer(xa, mods[0], norm_g[0, 0], rows_u, p0, tm_u)
    xa = _conv_ffn(xa, norm_g[0, 1], mods[0], rows_u, ff_wup[0], ff_conv[0], ff_convb[0], ff_wdown[0], tm_u)

    xl = xa.reshape(b, rows_u.per_batch, d)[:, lc:].reshape(rows_l.total, d)
    p1 = (ml_wdown[0], ml_qnorm[0], ml_kvnorm[0], ml_wuq[0], ml_wukv[0], ml_wo[0])
    xl = _mla_layer(xa, xl, mods[1], norm_g[1, 0], rows_u, rows_l, p1, tm_u, tm_l)
    xl = _conv_ffn(xl, norm_g[1, 1], mods[1], rows_l, ff_wup[1], ff_conv[1], ff_convb[1], ff_wdown[1], tm_l)
    return _final_norm(xl, final_g).reshape(b, ll, d)
```

```python
import functools
import math

import jax
import jax.numpy as jnp
from jax import lax
from jax.experimental import pallas as pl
from jax.experimental.pallas import tpu as pltpu

F32 = jnp.float32
BF16 = jnp.bfloat16

RMS_EPS = 1e-6
GN_EPS = 64e-5
QK_NOPE = 128
QK_ROPE = 64
V_HEAD = 128
GRID_W = 64
ROPE_THETA = 10000.0
LANES = 128
SUBLANES = 8
BF16_ROWS = 16
PITCH = 72
CHUNK = 64
CHUNK_HEADS = 16
VMEM_LIMIT = 56 * 1024 * 1024
MOD_ROWS = 8


def _cparams(n_axes):
    return pltpu.CompilerParams(dimension_semantics=("arbitrary",) * n_axes, vmem_limit_bytes=VMEM_LIMIT)


def _pick(m, cands):
    for c in cands:
        if m % c == 0:
            return c
    raise ValueError(f"no tile for {m} in {cands}")


def _round_up(x, m):
    return (x + m - 1) // m * m


class _Rows:
    def __init__(self, batch, n_ctx, n_lat, sub):
        self.batch, self.n_ctx, self.n_lat, self.sub = batch, n_ctx, n_lat, sub
        self.per_batch = n_ctx + n_lat
        self.total = batch * self.per_batch
        self.spb = self.per_batch // sub
        self.ctx_sub = n_ctx // sub

    def info(self, q):
        b = q // self.spb
        w = q - b * self.spb
        if self.ctx_sub:
            idx = jnp.where(w < self.ctx_sub, self.batch, b)
            start = (w == 0) | (w == self.ctx_sub)
            end = (w == self.ctx_sub - 1) | (w == self.spb - 1)
        else:
            idx = b
            start = w == 0
            end = w == self.spb - 1
        return idx, start, end


def _silu(x):
    return x * jax.nn.sigmoid(x)


def _rms(x):
    return x * lax.rsqrt(jnp.mean(x * x, axis=-1, keepdims=True) + RMS_EPS)


def _ada_kernel(c_ref, w_ref, b_ref, o_ref):
    o_ref[...] = jnp.dot(_silu(c_ref[...]), w_ref[...], preferred_element_type=F32) + b_ref[...]


def _ada(cvec, ada_w, ada_b):
    depth, d, n = ada_w.shape
    tn = _pick(n, (512, 256, 128))
    return pl.pallas_call(
        _ada_kernel,
        grid=(depth, n // tn),
        in_specs=[pl.BlockSpec((MOD_ROWS, d), lambda l, j: (0, 0)),
                  pl.BlockSpec((None, d, tn), lambda l, j: (l, 0, j)),
                  pl.BlockSpec((None, 1, tn), lambda l, j: (l, 0, j))],
        out_specs=pl.BlockSpec((None, MOD_ROWS, tn), lambda l, j: (l, 0, j)),
        out_shape=jax.ShapeDtypeStruct((depth, MOD_ROWS, n), F32),
        compiler_params=_cparams(2),
        name="ada_mod",
    )(cvec, ada_w, ada_b.reshape(depth, 1, n))


def _norm_mod_kernel(x_ref, g_ref, m_ref, o_ref, *, rows, tm, d, sh_col, sc_col):
    i = pl.program_id(0)
    nsub = tm // rows.sub
    for s in range(nsub):
        idx, _, _ = rows.info(i * nsub + s)
        sl = slice(s * rows.sub, (s + 1) * rows.sub)
        sh = m_ref[pl.ds(idx, 1), sh_col * d:(sh_col + 1) * d]
        sc = m_ref[pl.ds(idx, 1), sc_col * d:(sc_col + 1) * d]
        y = _rms(x_ref[sl, :]) * g_ref[...]
        o_ref[sl, :] = (y * (1.0 + sc) + sh).astype(o_ref.dtype)


def _norm_mod(x2, g, mods, rows, sh_col, sc_col, out_dtype):
    m, d = x2.shape
    tm = _pick(m, tuple(t for t in (512, 256, 128, 64, 32, 16, 8) if t % rows.sub == 0))
    return pl.pallas_call(
        functools.partial(_norm_mod_kernel, rows=rows, tm=tm, d=d, sh_col=sh_col, sc_col=sc_col),
        grid=(m // tm,),
        in_specs=[pl.BlockSpec((tm, d), lambda i: (i, 0)),
                  pl.BlockSpec((1, d), lambda i: (0, 0)),
                  pl.BlockSpec(mods.shape, lambda i: (0, 0))],
        out_specs=pl.BlockSpec((tm, d), lambda i: (i, 0)),
        out_shape=jax.ShapeDtypeStruct((m, d), out_dtype),
        compiler_params=_cparams(1),
        name="norm_mod",
    )(x2, g.reshape(1, d), mods)


def _final_norm_kernel(x_ref, g_ref, o_ref):
    o_ref[...] = _rms(x_ref[...]) * g_ref[...]


def _final_norm(x2, g):
    m, d = x2.shape
    tm = _pick(m, (512, 256, 128, 64))
    return pl.pallas_call(
        _final_norm_kernel,
        grid=(m // tm,),
        in_specs=[pl.BlockSpec((tm, d), lambda i: (i, 0)), pl.BlockSpec((1, d), lambda i: (0, 0))],
        out_specs=pl.BlockSpec((tm, d), lambda i: (i, 0)),
        out_shape=jax.ShapeDtypeStruct((m, d), F32),
        compiler_params=_cparams(1),
        name="final_norm",
    )(x2, g.reshape(1, d))


def _edge_masks(rows, i, tm):
    nsub = tm // rows.sub
    rid = lax.broadcasted_iota(jnp.int32, (tm, 1), 0)
    is_start = jnp.zeros((tm, 1), jnp.bool_)
    is_end = jnp.zeros((tm, 1), jnp.bool_)
    for s in range(nsub):
        _, st, en = rows.info(i * nsub + s)
        is_start = is_start | (rid == jnp.where(st, s * rows.sub, -1))
        is_end = is_end | (rid == jnp.where(en, (s + 1) * rows.sub - 1, -1))
    return rid, is_start, is_end


def _shifted(cur, prev_row, next_row, rid, is_start, is_end, tm):
    prev = jnp.where(rid == 0, prev_row, pltpu.roll(cur, 1, 0))
    nxt = jnp.where(rid == tm - 1, next_row, pltpu.roll(cur, tm - 1, 0))
    return jnp.where(is_start, 0.0, prev), jnp.where(is_end, 0.0, nxt)


def _rwkv_prep_kernel(x_ref, xp_ref, xn_ref, g_ref, m_ref, h_ref, xx_ref, *, rows, tm, d):
    i = pl.program_id(0)
    nsub = tm // rows.sub
    total_sub = rows.total // rows.sub

    def mod_rows(q):
        idx, _, _ = rows.info(q)
        return m_ref[pl.ds(idx, 1), 0:d], m_ref[pl.ds(idx, 1), d:2 * d]

    def h_of(x, q):
        sh, sc = mod_rows(q)
        return (_rms(x) * g_ref[...]) * (1.0 + sc) + sh

    for s in range(nsub):
        sl = slice(s * rows.sub, (s + 1) * rows.sub)
        h_ref[sl, :] = h_of(x_ref[sl, :], i * nsub + s)
    h = h_ref[...]
    hp = h_of(xp_ref[SUBLANES - 1:SUBLANES, :], jnp.maximum(i * nsub - 1, 0))
    hn = h_of(xn_ref[0:1, :], jnp.minimum((i + 1) * nsub, total_sub - 1))
    rid, is_start, is_end = _edge_masks(rows, i, tm)
    prev, nxt = _shifted(h, hp, hn, rid, is_start, is_end, tm)
    xx_ref[...] = 0.5 * (prev + nxt) - h


def _rwkv_prep(x2, g, mods, rows):
    m, d = x2.shape
    tm = max(_pick(m, (256, 128, 64)), rows.sub)
    r8 = tm // SUBLANES
    nblk8 = m // SUBLANES
    return pl.pallas_call(
        functools.partial(_rwkv_prep_kernel, rows=rows, tm=tm, d=d),
        grid=(m // tm,),
        in_specs=[pl.BlockSpec((tm, d), lambda i: (i, 0)),
                  pl.BlockSpec((SUBLANES, d), lambda i: (jnp.maximum(i * r8 - 1, 0), 0)),
                  pl.BlockSpec((SUBLANES, d), lambda i: (jnp.minimum((i + 1) * r8, nblk8 - 1), 0)),
                  pl.BlockSpec((1, d), lambda i: (0, 0)),
                  pl.BlockSpec(mods.shape, lambda i: (0, 0))],
        out_specs=[pl.BlockSpec((tm, d), lambda i: (i, 0)), pl.BlockSpec((tm, d), lambda i: (i, 0))],
        out_shape=[jax.ShapeDtypeStruct((m, d), F32), jax.ShapeDtypeStruct((m, d), F32)],
        compiler_params=_cparams(1),
        name="rwkv_prep",
    )(x2, x2, x2, g.reshape(1, d), mods)


def _mm_kernel(*refs, na, ne, no, prologue, epilogue, period):
    a_refs = refs[:na]
    w_ref = refs[na]
    e_refs = refs[na + 1:na + 1 + ne]
    o_refs = refs[na + 1 + ne:na + 1 + ne + no]
    a_scr = refs[-1]
    i = pl.program_id(0)
    j = pl.program_id(1)

    @pl.when(j % period == 0)
    def _():
        a_scr[...] = prologue(a_refs, i, j // period)

    acc = jnp.dot(a_scr[...], w_ref[...], preferred_element_type=F32)
    epilogue(acc, e_refs, o_refs, i, j)


def _mm(name, a_arrays, a_specs, prologue, w, e_arrays, e_specs, epilogue, out_shapes, out_specs,
        m, k, n, tm, tn, period=None):
    nj = n // tn
    period = nj if period is None else period
    return pl.pallas_call(
        functools.partial(_mm_kernel, na=len(a_arrays), ne=len(e_arrays), no=len(out_shapes),
                          prologue=prologue, epilogue=epilogue, period=period),
        grid=(m // tm, nj),
        in_specs=list(a_specs) + [pl.BlockSpec((k, tn), lambda i, j: (0, j))] + list(e_specs),
        out_specs=list(out_specs),
        out_shape=list(out_shapes),
        scratch_shapes=[pltpu.VMEM((tm, k), BF16)],
        compiler_params=_cparams(2),
        name=name,
    )(*a_arrays, w, *e_arrays)


def _gated_residual_epilogue(rows, tm):
    nsub = tm // rows.sub

    def epi(acc, e_refs, o_refs, i, j):
        res_ref, m_ref = e_refs
        for s in range(nsub):
            idx, _, _ = rows.info(i * nsub + s)
            sl = slice(s * rows.sub, (s + 1) * rows.sub)
            o_refs[0][sl, :] = res_ref[sl, :] + m_ref[pl.ds(idx, 1), :] * acc[sl, :]
    return epi


def _store_epilogue(acc, e_refs, o_refs, i, j):
    o_refs[0][...] = acc.astype(o_refs[0].dtype)


def _first_input(a_refs, i, grp):
    return a_refs[0][...]


def _proj_residual(name, a_arrays, a_specs, prologue, w, res, mods, rows, gate_col, tm, tn):
    m, d = res.shape
    k = w.shape[0]
    gate_blk = gate_col * (d // tn)
    return _mm(name, a_arrays, a_specs, prologue, w,
               [res, mods],
               [pl.BlockSpec((tm, tn), lambda i, j: (i, j)),
                pl.BlockSpec((MOD_ROWS, tn), lambda i, j: (0, gate_blk + j))],
               _gated_residual_epilogue(rows, tm),
               [jax.ShapeDtypeStruct((m, d), F32)],
               [pl.BlockSpec((tm, tn), lambda i, j: (i, j))],
               m, k, d, tm, tn)[0]


def _softplus(z):
    return jnp.maximum(z, 0.0) + jnp.log(1.0 + jnp.exp(-jnp.abs(z)))


def _pack_kernel(x_ref, o_ref, z_ref, *, b, h, n, tiles):
    hp = LANES // n
    if b * h < LANES:
        z_ref[...] = jnp.zeros_like(z_ref)
    for bb in range(b):
        for c in range(h // hp):
            t = x_ref[bb, :, c * LANES:(c + 1) * LANES].T
            for hh in range(hp):
                lane = bb * h + c * hp + hh
                z_ref[lane * PITCH:lane * PITCH + n, :] = t[hh * n:(hh + 1) * n, :]
    for k in range(n):
        m = z_ref[pl.ds(k, LANES, stride=PITCH), :].T
        if tiles:
            o_ref[k // SUBLANES, :, k % SUBLANES, :] = m
        else:
            o_ref[k] = m


def _pack(x2, b, lt, h, n, g0, groups, tiles):
    d = h * n
    tt = LANES
    x3 = x2.reshape(b, lt, x2.shape[1])
    if tiles:
        oshape, oblk = (groups, n // SUBLANES, lt, SUBLANES, LANES), (None, n // SUBLANES, tt, SUBLANES, LANES)
        omap = lambda g, ti: (g, 0, ti, 0, 0)
    else:
        oshape, oblk = (groups, n, lt, LANES), (None, n, tt, LANES)
        omap = lambda g, ti: (g, 0, ti, 0)
    return pl.pallas_call(
        functools.partial(_pack_kernel, b=b, h=h, n=n, tiles=tiles),
        grid=(groups, lt // tt),
        in_specs=[pl.BlockSpec((b, tt, d), lambda g, ti: (0, ti, g0 + g))],
        out_specs=pl.BlockSpec(oblk, omap),
        out_shape=jax.ShapeDtypeStruct(oshape, F32),
        scratch_shapes=[pltpu.VMEM((LANES * PITCH, tt), F32)],
        compiler_params=_cparams(2),
        name="lane_pack",
    )(x3)


def _unpack_kernel(o_ref, x_ref, z_ref, *, b, h, n):
    hp = LANES // n
    for k in range(n):
        z_ref[pl.ds(k, LANES, stride=PITCH), :] = o_ref[k // SUBLANES, :, k % SUBLANES, :].T
    for bb in range(b):
        for c in range(h // hp):
            parts = [z_ref[(bb * h + c * hp + hh) * PITCH:(bb * h + c * hp + hh) * PITCH + n, :] for hh in range(hp)]
            x_ref[bb, :, c * LANES:(c + 1) * LANES] = jnp.concatenate(parts, axis=0).T


def _unpack(o5, b, lt, h, n):
    d = h * n
    tt = LANES
    return pl.pallas_call(
        functools.partial(_unpack_kernel, b=b, h=h, n=n),
        grid=(lt // tt,),
        in_specs=[pl.BlockSpec((n // SUBLANES, tt, SUBLANES, LANES), lambda ti: (0, ti, 0, 0))],
        out_specs=pl.BlockSpec((b, tt, d), lambda ti: (0, ti, 0)),
        out_shape=jax.ShapeDtypeStruct((b, lt, d), F32),
        scratch_shapes=[pltpu.VMEM((LANES * PITCH, tt), F32)],
        compiler_params=_cparams(1),
        name="lane_unpack",
    )(o5).reshape(b * lt, d)


def _scan_kernel(rkf, vf, df, af, rkr, vr, dr, ar, kk_ref, ka_ref, yf_ref, yr_ref, s_ref, vec_ref,
                 *, tb, n, vchunk):
    j = pl.program_id(0)

    @pl.when(j == 0)
    def _():
        s_ref[...] = jnp.zeros_like(s_ref)

    kkc = kk_ref[...][:, None, :]
    kac = ka_ref[...][:, None, :]
    nvg = n // SUBLANES
    streams = ((rkf, vf, df, af, yf_ref), (rkr, vr, dr, ar, yr_ref))

    for dirn, (rk_, _, _, a_, _) in enumerate(streams):
        kt = rk_[1]
        at = a_[...]
        kx = kt * kkc
        nrm = jnp.sqrt(jnp.sum(kx * kx, axis=0, keepdims=True))
        kkn = kx / jnp.maximum(nrm, 1e-12)
        vec_ref[dirn, 0] = -kkn
        vec_ref[dirn, 1] = kkn * at
        vec_ref[dirn, 2] = kt * (1.0 + (at - 1.0) * kac)

    def bc(row):
        return jnp.broadcast_to(row, (SUBLANES, LANES))

    def tile(vg):
        return slice(vg * SUBLANES, (vg + 1) * SUBLANES)

    def make_step(dirn, rk_, v_, d_, y_):
        def step(i, sa):
            tt = i if dirn == 0 else tb - 1 - i
            tnext = jnp.minimum(tt + 1, tb - 1) if dirn == 0 else jnp.maximum(tt - 1, 0)
            sa_next = [None] * nvg
            for c in range(nvg // vchunk):
                vgs = [c * vchunk + q for q in range(vchunk)]
                vv = [v_[vg, tt] for vg in vgs]
                yy = [jnp.zeros((SUBLANES, LANES), F32) for _ in vgs]
                sn = [jnp.zeros((SUBLANES, LANES), F32) for _ in vgs]
                for kk in range(n):
                    wb = bc(d_[kk, pl.ds(tt, 1), :])
                    bb = bc(vec_ref[dirn, 1, kk, pl.ds(tt, 1), :])
                    kb = bc(vec_ref[dirn, 2, kk, pl.ds(tt, 1), :])
                    rb = bc(rk_[0, kk, pl.ds(tt, 1), :])
                    an = bc(vec_ref[dirn, 0, kk, pl.ds(tnext, 1), :])
                    for q, vg in enumerate(vgs):
                        snew = s_ref[dirn, kk, tile(vg), :] * wb + sa[vg] * bb + vv[q] * kb
                        s_ref[dirn, kk, tile(vg), :] = snew
                        yy[q] = yy[q] + snew * rb
                        sn[q] = sn[q] + snew * an
                for q, vg in enumerate(vgs):
                    y_[vg, tt] = yy[q]
                    sa_next[vg] = sn[q]
            return tuple(sa_next)
        return step

    for dirn, (rk_, v_, d_, _, y_) in enumerate(streams):
        t0 = 0 if dirn == 0 else tb - 1
        sa0 = [jnp.zeros((SUBLANES, LANES), F32) for _ in range(nvg)]
        for kk in range(n):
            a0 = bc(vec_ref[dirn, 0, kk, t0:t0 + 1, :])
            for vg in range(nvg):
                sa0[vg] = sa0[vg] + s_ref[dirn, kk, tile(vg), :] * a0
        lax.fori_loop(0, tb, make_step(dirn, rk_, v_, d_, y_), tuple(sa0))


def _rwkv_scan(rk, v, dec, icl, kk_c, ka_c, n_ctx, tb):
    _, n, lt, ln = rk.shape
    nvg = n // SUBLANES
    nb = lt // tb
    nc = n_ctx // tb

    def rev(j):
        return jnp.where(j < nc, nc - 1 - j, nb - 1 - (j - nc))

    def specs(tmap, dirn):
        return [pl.BlockSpec((2, n, tb, ln), lambda j: (0, 0, tmap(j), 0)),
                pl.BlockSpec((None, nvg, tb, SUBLANES, ln), lambda j: (0, 0, tmap(j), 0, 0)),
                pl.BlockSpec((None, n, tb, ln), lambda j: (dirn, 0, tmap(j), 0)),
                pl.BlockSpec((None, n, tb, ln), lambda j: (dirn, 0, tmap(j), 0))]

    ident = lambda j: j
    const = pl.BlockSpec((n, ln), lambda j: (0, 0))
    yshape = jax.ShapeDtypeStruct((nvg, lt, SUBLANES, ln), F32)
    return pl.pallas_call(
        functools.partial(_scan_kernel, tb=tb, n=n, vchunk=min(8, nvg)),
        grid=(nb,),
        in_specs=specs(ident, 0) + specs(rev, 1) + [const, const],
        out_specs=[pl.BlockSpec((nvg, tb, SUBLANES, ln), lambda j: (0, j, 0, 0)),
                   pl.BlockSpec((nvg, tb, SUBLANES, ln), lambda j: (0, rev(j), 0, 0))],
        out_shape=[yshape, yshape],
        scratch_shapes=[pltpu.VMEM((2, n, n, ln), F32), pltpu.VMEM((2, 3, n, tb, ln), F32)],
        compiler_params=_cparams(1),
        name="rwkv_scan",
    )(rk, v, dec, icl, rk, v, dec, icl, kk_c, ka_c)


def _readout_kernel(rk_ref, v_ref, a_ref, yf_ref, yr_ref, ka_ref, rkc_ref, gw_ref, gb_ref, o_ref, *, n):
    y = yf_ref[...] + yr_ref[...]
    mean = jnp.sum(jnp.sum(y, axis=0, keepdims=True), axis=2, keepdims=True) * (1.0 / n)
    yc = y - mean
    var = jnp.sum(jnp.sum(yc * yc, axis=0, keepdims=True), axis=2, keepdims=True) * (1.0 / n)
    yn = (yc * lax.rsqrt(var + GN_EPS)) * gw_ref[...][:, None] + gb_ref[...][:, None]
    r = rk_ref[0]
    k = rk_ref[1]
    kac = ka_ref[...][:, None, :]
    rrk = r * rkc_ref[...][:, None, :]
    kd0 = k * (1.0 + (a_ref[0] - 1.0) * kac)
    kd1 = k * (1.0 + (a_ref[1] - 1.0) * kac)
    bonus = jnp.sum(rrk * kd0, axis=0) + jnp.sum(rrk * kd1, axis=0)
    o_ref[...] = yn + bonus[None, :, None, :] * v_ref[...]


def _rwkv_readout(rk, v, icl, yf, yr, ka_c, rk_c, gw_t, gb_t, tb):
    _, n, lt, ln = rk.shape
    nvg = n // SUBLANES
    tile = pl.BlockSpec((nvg, tb, SUBLANES, ln), lambda j: (0, j, 0, 0))
    chan2 = pl.BlockSpec((2, n, tb, ln), lambda j: (0, 0, j, 0))
    const = pl.BlockSpec((n, ln), lambda j: (0, 0))
    const_t = pl.BlockSpec((nvg, SUBLANES, ln), lambda j: (0, 0, 0))
    return pl.pallas_call(
        functools.partial(_readout_kernel, n=n),
        grid=(lt // tb,),
        in_specs=[chan2, pl.BlockSpec((None, nvg, tb, SUBLANES, ln), lambda j: (0, 0, j, 0, 0)), chan2, tile, tile,
                  const, const, const_t, const_t],
        out_specs=tile,
        out_shape=jax.ShapeDtypeStruct((nvg, lt, SUBLANES, ln), F32),
        compiler_params=_cparams(1),
        name="rwkv_readout",
    )(rk, v, icl, yf, yr, ka_c, rk_c, gw_t, gb_t)


NT_DIMS = (((1,), (1,)), ((), ()))


def _split_dot(x, ones, terms, ones_left=False):
    acc = None
    rem = x
    for _ in range(terms):
        piece = rem.astype(BF16)
        part = (jnp.dot(ones, piece, preferred_element_type=F32) if ones_left
                else jnp.dot(piece, ones, preferred_element_type=F32))
        acc = part if acc is None else acc + part
        rem = rem - piece.astype(F32)
    return acc
TN_DIMS = (((0,), (0,)), ((), ()))


def _chunk_kernel(rf, kf, vf, lwf, icf, rr, kr, vr, lwr, icr, kk_ref, ka_ref, yf_ref, yr_ref, s_ref, *, c, n, hb):
    j = pl.program_id(2)

    @pl.when(j == 0)
    def _():
        s_ref[...] = jnp.zeros_like(s_ref)

    row = lax.broadcasted_iota(jnp.int32, (c, c), 0)
    col = lax.broadcasted_iota(jnp.int32, (c, c), 1)
    seg = (lax.broadcasted_iota(jnp.int32, (LANES, LANES), 0) // n
           == lax.broadcasted_iota(jnp.int32, (LANES, LANES), 1) // n).astype(F32).astype(BF16)
    hp = LANES // n
    doublings = int(math.log2(c))
    streams = ((rf, kf, vf, lwf, icf, yf_ref, col <= row, col < row, c - 1),
               (rr, kr, vr, lwr, icr, yr_ref, col >= row, col > row, 0))
    chains = []
    for dirn, (r_, k_, v_, lw_, ic_, y_, incl, strict, last) in enumerate(streams):
        tri = incl.astype(F32).astype(BF16)
        for p in range(hb // hp):
            sl = slice(p * LANES, (p + 1) * LANES)
            r, k, v, lw, ic = r_[:, sl], k_[:, sl], v_[:, sl], lw_[:, sl], ic_[:, sl]
            kx = k * kk_ref[:, sl]
            nrm = jnp.sqrt(_split_dot(kx * kx, seg, 3))
            kkn = kx / jnp.maximum(nrm, 1e-12)
            g = _split_dot(lw, tri, 3, ones_left=True)
            eig = jnp.exp(-g)
            rt = r * jnp.exp(g)
            at = -kkn * jnp.exp(g - lw)
            kt = (k * (1.0 + (ic - 1.0) * ka_ref[:, sl])) * eig
            bt = (kkn * ic) * eig
            eglast = jnp.exp(g[last:last + 1, :])
            for hh in range(hp):
                ls = slice(hh * n, (hh + 1) * n)
                chains.append(dict(
                    dirn=dirn, head=p * hp + hh, strict=strict, incl=incl, eg=eglast[:, ls], vh=v[:, ls].astype(BF16),
                    pm=jnp.concatenate([rt[:, ls], at[:, ls]], axis=0).astype(BF16),
                    qm=jnp.concatenate([kt[:, ls], bt[:, ls]], axis=0).astype(BF16)))
    for ch in chains:
        gm = lax.dot_general(ch["pm"], ch["qm"], NT_DIMS, preferred_element_type=F32)
        ch["a_rk"] = jnp.where(ch["incl"], gm[:c, :c], 0.0).astype(BF16)
        ch["a_rb"] = jnp.where(ch["incl"], gm[:c, c:], 0.0).astype(BF16)
        ch["a_ak"] = jnp.where(ch["strict"], gm[c:, :c], 0.0).astype(BF16)
        ch["m"] = jnp.where(ch["strict"], gm[c:, c:], 0.0)
    for ch in chains:
        ch["s0"] = s_ref[ch["dirn"], ch["head"]]
        ch["ps"] = lax.dot_general(ch["pm"], ch["s0"].astype(BF16), NT_DIMS, preferred_element_type=F32)
    for ch in chains:
        ch["x"] = ch["ps"][c:] + jnp.dot(ch["a_ak"], ch["vh"], preferred_element_type=F32)
        ch["y0"] = ch["ps"][:c] + jnp.dot(ch["a_rk"], ch["vh"], preferred_element_type=F32)
    for it in range(doublings):
        for ch in chains:
            m, x = ch["m"], ch["x"]
            mb, xb = m.astype(BF16), x.astype(BF16)
            xl = (x - xb.astype(F32)).astype(BF16)
            ch["x"] = x + (jnp.dot(mb, xb, preferred_element_type=F32) + jnp.dot(mb, xl, preferred_element_type=F32))
            if it + 1 < doublings:
                ch["m"] = jnp.dot(mb, mb, preferred_element_type=F32)
    for ch in chains:
        ub = ch["x"].astype(BF16)
        ch["y"] = ch["y0"] + jnp.dot(ch["a_rb"], ub, preferred_element_type=F32)
        vu = jnp.concatenate([ch["vh"], ub], axis=0)
        upd = lax.dot_general(vu, ch["qm"], TN_DIMS, preferred_element_type=F32)
        s_ref[ch["dirn"], ch["head"]] = (ch["s0"] + upd) * ch["eg"]
    for dirn, y_ in enumerate((yf_ref, yr_ref)):
        ys = [ch["y"] for ch in chains if ch["dirn"] == dirn]
        y_[...] = jnp.concatenate(ys, axis=1)


def _rwkv_chunked(rkv, logw, icl, k_k, k_a, b, lt, n_ctx, h, n):
    d = h * n
    c = CHUNK
    hb = min(CHUNK_HEADS, h)
    nch, nc, ng, w = lt // c, n_ctx // c, h // hb, hb * n

    def rev(j):
        return jnp.where(j < nc, nc - 1 - j, nch - 1 - (j - nc))

    def spec(tmap, colgrp):
        return pl.BlockSpec((c, w), lambda bb, g, j: (bb * nch + tmap(j), colgrp * ng + g))

    ident = lambda j: j
    const = pl.BlockSpec((1, w), lambda bb, g, j: (0, g))
    return pl.pallas_call(
        functools.partial(_chunk_kernel, c=c, n=n, hb=hb),
        grid=(b, ng, nch),
        in_specs=[spec(ident, 0), spec(ident, 1), spec(ident, 2), spec(ident, 0), spec(ident, 0),
                  spec(rev, 0), spec(rev, 1), spec(rev, 2), spec(rev, 1), spec(rev, 1), const, const],
        out_specs=[spec(ident, 0), spec(rev, 0)],
        out_shape=[jax.ShapeDtypeStruct((b * lt, d), F32)] * 2,
        scratch_shapes=[pltpu.VMEM((2, hb, n, n), F32)],
        compiler_params=_cparams(3),
        name="rwkv_chunk",
    )(rkv, rkv, rkv, logw, icl, rkv, rkv, rkv, logw, icl, k_k.reshape(1, d), k_a.reshape(1, d))


def _readout_tok_kernel(yf_ref, yr_ref, r_ref, k_ref, v_ref, a0_ref, a1_ref, g_ref, ka_ref, rk_ref, gw_ref, gb_ref,
                        o_ref, *, n, d):
    seg = (lax.broadcasted_iota(jnp.int32, (LANES, LANES), 0) // n
           == lax.broadcasted_iota(jnp.int32, (LANES, LANES), 1) // n).astype(F32).astype(BF16)

    def seg_sum(t):
        return _split_dot(t, seg, 2)

    for p in range(d // LANES):
        sl = slice(p * LANES, (p + 1) * LANES)
        y = yf_ref[:, sl] + yr_ref[:, sl]
        yc = y - seg_sum(y) * (1.0 / n)
        var = seg_sum(yc * yc) * (1.0 / n)
        yn = (yc * lax.rsqrt(var + GN_EPS)) * gw_ref[:, sl] + gb_ref[:, sl]
        k = k_ref[:, sl]
        ka = ka_ref[:, sl]
        kd = k * (1.0 + (a0_ref[:, sl] - 1.0) * ka) + k * (1.0 + (a1_ref[:, sl] - 1.0) * ka)
        bonus = seg_sum(r_ref[:, sl] * rk_ref[:, sl] * kd)
        o_ref[:, sl] = ((yn + bonus * v_ref[:, sl]) * g_ref[:, sl]).astype(o_ref.dtype)


def _rwkv_readout_tok(yf, yr, rkv, icl, gate, k_a, r_k, gn_w, gn_b, n):
    m, d = yf.shape
    tm = _pick(m, (128, 64))
    blk = lambda cg: pl.BlockSpec((tm, d), lambda i: (i, cg))
    const = pl.BlockSpec((1, d), lambda i: (0, 0))
    return pl.pallas_call(
        functools.partial(_readout_tok_kernel, n=n, d=d),
        grid=(m // tm,),
        in_specs=[blk(0), blk(0), blk(0), blk(1), blk(2), blk(0), blk(1), blk(0)] + [const] * 4,
        out_specs=blk(0),
        out_shape=jax.ShapeDtypeStruct((m, d), BF16),
        compiler_params=_cparams(1),
        name="rwkv_readout",
    )(yf, yr, rkv, rkv, rkv, icl, icl, gate, k_a.reshape(1, d), r_k.reshape(1, d), gn_w.reshape(1, d), gn_b.reshape(1, d))


def _ffn_up_kernel(h_ref, hp_ref, hn_ref, wg_ref, wv_ref, cw_ref, cb_ref, o_ref, *, rows, tm):
    i = pl.program_id(0)
    h = h_ref[...]
    wg = wg_ref[...]
    gate = jnp.dot(h, wg, preferred_element_type=F32)
    val = jnp.dot(h, wv_ref[...], preferred_element_type=F32)
    gp = jnp.dot(hp_ref[...], wg, preferred_element_type=F32)[BF16_ROWS - 1:BF16_ROWS, :]
    gn = jnp.dot(hn_ref[...], wg, preferred_element_type=F32)[0:1, :]
    rid, is_start, is_end = _edge_masks(rows, i, tm)
    prev, nxt = _shifted(gate, gp, gn, rid, is_start, is_end, tm)
    conv = prev * cw_ref[0:1, :] + gate * cw_ref[1:2, :] + nxt * cw_ref[2:3, :] + cb_ref[...]
    o_ref[...] = (_silu(conv) * val).astype(o_ref.dtype)


def _ffn_up(hf, wup, conv_w, conv_b, rows, tm, tn):
    m, d = hf.shape
    f = wup.shape[1] // 2
    r8 = tm // BF16_ROWS
    nblk8 = m // BF16_ROWS
    njf = f // tn
    cw = jnp.zeros((SUBLANES, f), F32).at[:conv_w.shape[0]].set(conv_w)
    return pl.pallas_call(
        functools.partial(_ffn_up_kernel, rows=rows, tm=tm),
        grid=(m // tm, njf),
        in_specs=[pl.BlockSpec((tm, d), lambda i, j: (i, 0)),
                  pl.BlockSpec((BF16_ROWS, d), lambda i, j: (jnp.maximum(i * r8 - 1, 0), 0)),
                  pl.BlockSpec((BF16_ROWS, d), lambda i, j: (jnp.minimum((i + 1) * r8, nblk8 - 1), 0)),
                  pl.BlockSpec((d, tn), lambda i, j: (0, j)),
                  pl.BlockSpec((d, tn), lambda i, j: (0, j + njf)),
                  pl.BlockSpec((SUBLANES, tn), lambda i, j: (0, j)),
                  pl.BlockSpec((1, tn), lambda i, j: (0, j))],
        out_specs=pl.BlockSpec((tm, tn), lambda i, j: (i, j)),
        out_shape=jax.ShapeDtypeStruct((m, f), BF16),
        compiler_params=_cparams(2),
        name="ffn_up",
    )(hf, hf, hf, wup, wup, cw, conv_b.reshape(1, f))


def _conv_ffn(x2, g, mods, rows, wup, conv_w, conv_b, wdown, tm):
    m, d = x2.shape
    f = wdown.shape[0]
    hf = _norm_mod(x2, g, mods, rows, 3, 4, BF16)
    act = _ffn_up(hf, wup.astype(BF16), conv_w, conv_b, rows, tm, _pick(f, (512, 256, 128)))
    tmd = _pick(m, tuple(t for t in (512, 256, 128, 64) if t <= tm and t % rows.sub == 0))
    return _proj_residual("ffn_down", [act], [pl.BlockSpec((tmd, f), lambda i, j: (i, 0))],
                          _first_input, wdown.astype(BF16), x2, mods, rows, 5,
                          tmd, _pick(d, (512, 256, 128)))


def _rope(x, cos, sin_signed):
    lane = lax.broadcasted_iota(jnp.int32, x.shape, 1)
    first = (lane & 31) < 16
    swapped = jnp.where(first, pltpu.roll(x, LANES - 16, 1), pltpu.roll(x, 16, 1))
    return x * cos + swapped * sin_signed


def _kv_up_kernel(c_ref, wk_ref, wv_ref, kr_ref, ko_ref, vo_ref, *, hpt):
    c = c_ref[...]
    kn = jnp.dot(c, wk_ref[...], preferred_element_type=F32).astype(BF16)
    vo_ref[...] = jnp.dot(c, wv_ref[...], preferred_element_type=F32).astype(BF16)
    kr = kr_ref[...]
    for h in range(hpt):
        ko_ref[:, h * 2 * LANES:h * 2 * LANES + QK_NOPE] = kn[:, h * QK_NOPE:(h + 1) * QK_NOPE]
        ko_ref[:, h * 2 * LANES + QK_NOPE:(h + 1) * 2 * LANES] = kr


def _attn_kernel(q_ref, k_ref, v_ref, o_ref, *, hpa, hq):
    for h in range(hpa):
        q = q_ref[:, h * hq:(h + 1) * hq]
        k = k_ref[:, h * hq:(h + 1) * hq]
        s = lax.dot_general(q, k, (((1,), (1,)), ((), ())), preferred_element_type=F32)
        p = jnp.exp2(s - jnp.max(s, axis=-1, keepdims=True))
        denom = jnp.sum(p, axis=-1, keepdims=True)
        o = jnp.dot(p.astype(BF16), v_ref[:, h * V_HEAD:(h + 1) * V_HEAD], preferred_element_type=F32)
        o_ref[:, h * V_HEAD:(h + 1) * V_HEAD] = (o / denom).astype(o_ref.dtype)


def _head_const(p_hn, b):
    h, n = p_hn.shape
    y = jnp.broadcast_to(p_hn.T[:, None, :], (n, b, h)).reshape(n, b * h)
    return jnp.pad(y, ((0, 0), (0, LANES - b * h)))


def _rwkv_layer(xa, mods, norm_g, rows, p, tm):
    (mu, wr, wk, wv, wo, w0, w1, w2, a0, a1, a2, g1, g2, k_k, k_a, r_k, gn_w, gn_b) = p
    m, d = xa.shape
    h_heads, n = r_k.shape
    b, lt = rows.batch, rows.per_batch
    glora, dlora, alora = g1.shape[1], w1.shape[2], a1.shape[2]
    gw = _round_up(max(glora, 2 * dlora, 2 * alora), LANES)

    h, xx = _rwkv_prep(xa, norm_g, mods, rows)

    a_specs = [pl.BlockSpec((tm, d), lambda i, j: (i, 0)), pl.BlockSpec((tm, d), lambda i, j: (i, 0)),
               pl.BlockSpec((MOD_ROWS, d), lambda i, j: (0, 0))]

    def mix(a, i, grp):
        return (a[0][...] + a[1][...] * a[2][pl.ds(grp, 1), :]).astype(BF16)

    def pad_rows(t):
        return jnp.zeros((MOD_ROWS, d), F32).at[:t.shape[0]].set(t)

    tn = _pick(d, (512, 256, 128))
    w_rkv = jnp.concatenate([wr, wk, wv], axis=1).astype(BF16)
    rkv = _mm("rwkv_rkv", [h, xx, pad_rows(mu[jnp.array([0, 2, 3])])], a_specs, mix, w_rkv, [], [],
              _store_epilogue,
              [jax.ShapeDtypeStruct((m, 3 * d), F32)], [pl.BlockSpec((tm, tn), lambda i, j: (i, j))],
              m, d, 3 * d, tm, tn, period=d // tn)[0]

    def pad_cols(t):
        return jnp.pad(t, ((0, 0), (0, gw - t.shape[1])))

    w_lora = jnp.concatenate([pad_cols(g1), pad_cols(jnp.concatenate([w1[0], w1[1]], axis=1)),
                              pad_cols(jnp.concatenate([a1[0], a1[1]], axis=1))], axis=1).astype(BF16)

    def lora_act(acc, e, o, i, j):
        o[0][...] = jnp.where(j == 0, jax.nn.sigmoid(acc), jnp.where(j == 1, jnp.tanh(acc), acc)).astype(BF16)

    hid = _mm("rwkv_lora1", [h, xx, pad_rows(mu[jnp.array([5, 1, 4])])], a_specs, mix, w_lora, [], [], lora_act,
              [jax.ShapeDtypeStruct((m, 3 * gw), BF16)], [pl.BlockSpec((tm, gw), lambda i, j: (i, j))],
              m, d, 3 * gw, tm, gw, period=1)[0]

    def hid_spec(grp):
        return [pl.BlockSpec((tm, gw), lambda i, j: (i, grp))]

    ident = _first_input
    g2p = jnp.zeros((gw, d), F32).at[:glora].set(g2).astype(BF16)
    gate = _mm("rwkv_gate", [hid], hid_spec(0), ident, g2p, [], [],
               _store_epilogue,
               [jax.ShapeDtypeStruct((m, d), F32)], [pl.BlockSpec((tm, tn), lambda i, j: (i, j))],
               m, gw, d, tm, tn)[0]

    def two_dir(t2, lora):
        z = jnp.zeros((gw, 2 * d), F32)
        z = z.at[:lora, :d].set(t2[0]).at[lora:2 * lora, d:].set(t2[1])
        return z.astype(BF16)

    bias_spec = [pl.BlockSpec((1, tn), lambda i, j: (0, j))]

    def decay_epi(acc, e, o, i, j):
        w_log = -_softplus(-(e[0][...] + acc)) - 0.5
        o[0][...] = -jnp.exp(w_log)

    logw = _mm("rwkv_decay", [hid], hid_spec(1), ident, two_dir(w2, dlora), [w0.reshape(1, 2 * d)], bias_spec,
               decay_epi, [jax.ShapeDtypeStruct((m, 2 * d), F32)], [pl.BlockSpec((tm, tn), lambda i, j: (i, j))],
               m, gw, 2 * d, tm, tn)[0]

    def iclr_epi(acc, e, o, i, j):
        o[0][...] = jax.nn.sigmoid(e[0][...] + acc)

    icl = _mm("rwkv_iclr", [hid], hid_spec(2), ident, two_dir(a2, alora), [a0.reshape(1, 2 * d)], bias_spec,
              iclr_epi, [jax.ShapeDtypeStruct((m, 2 * d), F32)], [pl.BlockSpec((tm, tn), lambda i, j: (i, j))],
              m, gw, 2 * d, tm, tn)[0]

    yf, yr = _rwkv_chunked(rkv, logw, icl, k_k, k_a, b, lt, rows.n_ctx, h_heads, n)
    og = _rwkv_readout_tok(yf, yr, rkv, icl, gate, k_a, r_k.reshape(d), gn_w, gn_b, n)
    return _proj_residual("rwkv_out", [og], [pl.BlockSpec((tm, d), lambda i, j: (i, 0))], _first_input,
                          wo.astype(BF16), xa, mods, rows, 2, tm, tn)


def _mla_layer(xa, xl, mods, norm_g, rows_u, rows_l, p, tm_u, tm_l):
    wdown, qnorm, kvnorm, wuq, wukv, wo = p
    m, d = xa.shape
    b, lt, lc, ll = rows_u.batch, rows_u.per_batch, rows_u.n_ctx, rows_u.n_lat
    ql, kvl = qnorm.shape[0], kvnorm.shape[0]
    heads = wuq.shape[1] // (QK_NOPE + QK_ROPE)
    hq = 2 * LANES

    npairs = QK_ROPE // 4
    pos = jnp.arange(ll)
    inv_freq = jnp.float32(ROPE_THETA) ** (-jnp.arange(npairs, dtype=F32) / npairs)
    ang_r = (pos // GRID_W).astype(F32)[:, None] * inv_freq
    ang_c = (pos % GRID_W).astype(F32)[:, None] * inv_freq
    cos64 = jnp.concatenate([jnp.cos(ang_r)] * 2 + [jnp.cos(ang_c)] * 2, axis=1)
    sin64 = jnp.concatenate([-jnp.sin(ang_r), jnp.sin(ang_r), -jnp.sin(ang_c), jnp.sin(ang_c)], axis=1)
    padl = ((lc, 0), (0, LANES - QK_ROPE))
    cos_t = jnp.tile(jnp.pad(cos64, padl, constant_values=1.0), (b, 1))
    sin_t = jnp.tile(jnp.pad(sin64, padl), (b, 1))

    h1 = _norm_mod(xa, norm_g, mods, rows_u, 0, 1, BF16)

    nd = _round_up(ql + kvl + QK_ROPE, LANES)
    wd = jnp.pad(wdown, ((0, 0), (0, nd - wdown.shape[1]))).astype(BF16)

    def down_epi(acc, e, o, i, j):
        qn_ref, kvn_ref, cos_ref, sin_ref = e
        o[0][...] = (_rms(acc[:, :ql]) * qn_ref[...]).astype(BF16)
        o[1][...] = (_rms(acc[:, ql:ql + kvl]) * kvn_ref[...]).astype(BF16)
        o[2][...] = _rope(acc[:, ql + kvl:], cos_ref[...], sin_ref[...]).astype(BF16)

    tab_spec = pl.BlockSpec((tm_u, LANES), lambda i, j: (i, 0))
    cq, ckv, krope = _mm(
        "mla_down", [h1], [pl.BlockSpec((tm_u, d), lambda i, j: (i, 0))], _first_input, wd,
        [qnorm.reshape(1, ql), kvnorm.reshape(1, kvl), cos_t, sin_t],
        [pl.BlockSpec((1, ql), lambda i, j: (0, 0)), pl.BlockSpec((1, kvl), lambda i, j: (0, 0)), tab_spec, tab_spec],
        down_epi,
        [jax.ShapeDtypeStruct((m, ql), BF16), jax.ShapeDtypeStruct((m, kvl), BF16),
         jax.ShapeDtypeStruct((m, LANES), BF16)],
        [pl.BlockSpec((tm_u, ql), lambda i, j: (i, 0)), pl.BlockSpec((tm_u, kvl), lambda i, j: (i, 0)),
         pl.BlockSpec((tm_u, LANES), lambda i, j: (i, 0))],
        m, d, nd, tm_u, nd)

    wq = wuq.reshape(ql, heads, QK_NOPE + QK_ROPE)
    wq = jnp.pad(wq, ((0, 0), (0, 0), (0, hq - QK_NOPE - QK_ROPE))).reshape(ql, heads * hq).astype(BF16)
    hpt = _pick(heads, (4, 2, 1))

    qscale = (QK_NOPE + QK_ROPE) ** -0.5 * math.log2(math.e)

    def q_epi(acc, e, o, i, j):
        cos_ref, sin_ref = e
        acc = acc * qscale
        for hh in range(hpt):
            lo = hh * hq
            o[0][:, lo:lo + QK_NOPE] = acc[:, lo:lo + QK_NOPE].astype(BF16)
            o[0][:, lo + QK_NOPE:lo + hq] = _rope(acc[:, lo + QK_NOPE:lo + hq], cos_ref[...], sin_ref[...]).astype(BF16)

    q = _mm("mla_q_up", [cq], [pl.BlockSpec((tm_u, ql), lambda i, j: (i, 0))], _first_input, wq,
            [cos_t, sin_t], [tab_spec, tab_spec], q_epi,
            [jax.ShapeDtypeStruct((m, heads * hq), BF16)], [pl.BlockSpec((tm_u, hpt * hq), lambda i, j: (i, j))],
            m, ql, heads * hq, tm_u, hpt * hq)[0]

    wkv = wukv.reshape(kvl, heads, QK_NOPE + V_HEAD)
    wk = wkv[:, :, :QK_NOPE].reshape(kvl, heads * QK_NOPE).astype(BF16)
    wv = wkv[:, :, QK_NOPE:].reshape(kvl, heads * V_HEAD).astype(BF16)
    kcat, vals = pl.pallas_call(
        functools.partial(_kv_up_kernel, hpt=hpt),
        grid=(m // tm_u, heads // hpt),
        in_specs=[pl.BlockSpec((tm_u, kvl), lambda i, j: (i, 0)),
                  pl.BlockSpec((kvl, hpt * QK_NOPE), lambda i, j: (0, j)),
                  pl.BlockSpec((kvl, hpt * V_HEAD), lambda i, j: (0, j)),
                  pl.BlockSpec((tm_u, LANES), lambda i, j: (i, 0))],
        out_specs=[pl.BlockSpec((tm_u, hpt * hq), lambda i, j: (i, j)),
                   pl.BlockSpec((tm_u, hpt * V_HEAD), lambda i, j: (i, j))],
        out_shape=[jax.ShapeDtypeStruct((m, heads * hq), BF16), jax.ShapeDtypeStruct((m, heads * V_HEAD), BF16)],
        compiler_params=_cparams(2),
        name="mla_kv_up",
    )(ckv, wk, wv, krope)

    tq = _pick(math.gcd(lc, ll), (256, 128, 64))
    hpa = _pick(heads, (2, 1))
    o = pl.pallas_call(
        functools.partial(_attn_kernel, hpa=hpa, hq=hq),
        grid=(b, heads // hpa, ll // tq),
        in_specs=[pl.BlockSpec((None, tq, hpa * hq), lambda bb, hh, i: (bb, lc // tq + i, hh)),
                  pl.BlockSpec((None, lt, hpa * hq), lambda bb, hh, i: (bb, 0, hh)),
                  pl.BlockSpec((None, lt, hpa * V_HEAD), lambda bb, hh, i: (bb, 0, hh))],
        out_specs=pl.BlockSpec((None, tq, hpa * V_HEAD), lambda bb, hh, i: (bb, i, hh)),
        out_shape=jax.ShapeDtypeStruct((b, ll, heads * V_HEAD), BF16),
        compiler_params=_cparams(3),
        name="mla_attn",
    )(q.reshape(b, lt, heads * hq), kcat.reshape(b, lt, heads * hq), vals.reshape(b, lt, heads * V_HEAD))

    ko = heads * V_HEAD
    return _proj_residual("mla_out", [o.reshape(b * ll, ko)], [pl.BlockSpec((tm_l, ko), lambda i, j: (i, 0))],
                          _first_input, wo.astype(BF16), xl, mods, rows_l, 2,
                          tm_l, _pick(d, (512, 256, 128)))


def kernel(x, c, ctx, c_ctx, ada_w, ada_b, norm_g, final_g, rk_mu, rk_wr, rk_wk, rk_wv, rk_wo, rk_w0, rk_w1, rk_w2, rk_a0, rk_a1, rk_a2, rk_g1, rk_g2, rk_kk, rk_ka, rk_rk, rk_gn_w, rk_gn_b, ml_wdown, ml_qnorm, ml_kvnorm, ml_wuq, ml_wukv, ml_wo, ff_wup, ff_conv, ff_convb, ff_wdown):
    b, ll, d = x.shape
    lc = ctx.shape[1]
    depth = ada_w.shape[0]
    heads_r = rk_rk.shape[1]
    assert depth == 2 and rk_mu.shape[0] == 1 and ml_wdown.shape[0] == 1, "one RWKV layer then one MLA layer"
    assert b + 1 <= MOD_ROWS and b * heads_r <= LANES
    sub = _pick(math.gcd(lc, ll), (256, 128, 64, 32, 16, 8))
    rows_u = _Rows(b, lc, ll, sub)
    rows_l = _Rows(b, 0, ll, sub)
    tm_u = _pick(rows_u.total, tuple(t for t in (1024, 512, 256, 128, 64) if t % sub == 0))
    tm_l = _pick(rows_l.total, tuple(t for t in (1024, 512, 256, 128, 64) if t % sub == 0))

    cvec = jnp.zeros((MOD_ROWS, d), F32).at[:b].set(c).at[b].set(c_ctx)
    mods = _ada(cvec, ada_w, ada_b)

    xa = jnp.concatenate([ctx, x], axis=1).reshape(rows_u.total, d)

    p0 = (rk_mu[0], rk_wr[0], rk_wk[0], rk_wv[0], rk_wo[0], rk_w0[0], rk_w1[0], rk_w2[0], rk_a0[0], rk_a1[0],
          rk_a2[0], rk_g1[0], rk_g2[0], rk_kk[0], rk_ka[0], rk_rk[0], rk_gn_w[0], rk_gn_b[0])
    xa = _rwkv_layer(xa, mods[0], norm_g[0, 0], rows_u, p0, tm_u)
    xa = _conv_ffn(xa, norm_g[0, 1], mods[0], rows_u, ff_wup[0], ff_conv[0], ff_convb[0], ff_wdown[0], tm_u)

    xl = xa.reshape(b, rows_u.per_batch, d)[:, lc:].reshape(rows_l.total, d)
    p1 = (ml_wdown[0], ml_qnorm[0], ml_kvnorm[0], ml_wuq[0], ml_wukv[0], ml_wo[0])
    xl = _mla_layer(xa, xl, mods[1], norm_g[1, 0], rows_u, rows_l, p1, tm_u, tm_l)
    xl = _conv_ffn(xl, norm_g[1, 1], mods[1], rows_l, ff_wup[1], ff_conv[1], ff_convb[1], ff_wdown[1], tm_l)
    return _final_norm(xl, final_g).reshape(b, ll, d)
```

```python
import functools
import math

import jax
import jax.numpy as jnp
from jax import lax
from jax.experimental import pallas as pl
from jax.experimental.pallas import tpu as pltpu

F32 = jnp.float32
BF16 = jnp.bfloat16

RMS_EPS = 1e-6
GN_EPS = 64e-5
QK_NOPE = 128
QK_ROPE = 64
V_HEAD = 128
GRID_W = 64
ROPE_THETA = 10000.0
LANES = 128
SUBLANES = 8
BF16_ROWS = 16
PITCH = 72
CHUNK = 64
CHUNK_HEADS = 32
VMEM_LIMIT = 56 * 1024 * 1024
MOD_ROWS = 8


def _cparams(n_axes):
    return pltpu.CompilerParams(dimension_semantics=("arbitrary",) * n_axes, vmem_limit_bytes=VMEM_LIMIT)


def _pick(m, cands):
    for c in cands:
        if m % c == 0:
            return c
    raise ValueError(f"no tile for {m} in {cands}")


def _round_up(x, m):
    return (x + m - 1) // m * m


class _Rows:
    def __init__(self, batch, n_ctx, n_lat, sub):
        self.batch, self.n_ctx, self.n_lat, self.sub = batch, n_ctx, n_lat, sub
        self.per_batch = n_ctx + n_lat
        self.total = batch * self.per_batch
        self.spb = self.per_batch // sub
        self.ctx_sub = n_ctx // sub

    def info(self, q):
        b = q // self.spb
        w = q - b * self.spb
        if self.ctx_sub:
            idx = jnp.where(w < self.ctx_sub, self.batch, b)
            start = (w == 0) | (w == self.ctx_sub)
            end = (w == self.ctx_sub - 1) | (w == self.spb - 1)
        else:
            idx = b
            start = w == 0
            end = w == self.spb - 1
        return idx, start, end


def _silu(x):
    return x * jax.nn.sigmoid(x)


def _rms(x):
    return x * lax.rsqrt(jnp.mean(x * x, axis=-1, keepdims=True) + RMS_EPS)


def _ada_kernel(c_ref, w_ref, b_ref, o_ref):
    o_ref[...] = jnp.dot(_silu(c_ref[...]), w_ref[...], preferred_element_type=F32) + b_ref[...]


def _ada(cvec, ada_w, ada_b):
    depth, d, n = ada_w.shape
    tn = _pick(n, (512, 256, 128))
    return pl.pallas_call(
        _ada_kernel,
        grid=(depth, n // tn),
        in_specs=[pl.BlockSpec((MOD_ROWS, d), lambda l, j: (0, 0)),
                  pl.BlockSpec((None, d, tn), lambda l, j: (l, 0, j)),
                  pl.BlockSpec((None, 1, tn), lambda l, j: (l, 0, j))],
        out_specs=pl.BlockSpec((None, MOD_ROWS, tn), lambda l, j: (l, 0, j)),
        out_shape=jax.ShapeDtypeStruct((depth, MOD_ROWS, n), F32),
        compiler_params=_cparams(2),
        name="ada_mod",
    )(cvec, ada_w, ada_b.reshape(depth, 1, n))


def _norm_mod_kernel(x_ref, g_ref, m_ref, o_ref, *, rows, tm, d, sh_col, sc_col):
    i = pl.program_id(0)
    nsub = tm // rows.sub
    for s in range(nsub):
        idx, _, _ = rows.info(i * nsub + s)
        sl = slice(s * rows.sub, (s + 1) * rows.sub)
        sh = m_ref[pl.ds(idx, 1), sh_col * d:(sh_col + 1) * d]
        sc = m_ref[pl.ds(idx, 1), sc_col * d:(sc_col + 1) * d]
        y = _rms(x_ref[sl, :]) * g_ref[...]
        o_ref[sl, :] = (y * (1.0 + sc) + sh).astype(o_ref.dtype)


def _norm_mod(x2, g, mods, rows, sh_col, sc_col, out_dtype):
    m, d = x2.shape
    tm = _pick(m, tuple(t for t in (512, 256, 128, 64, 32, 16, 8) if t % rows.sub == 0))
    return pl.pallas_call(
        functools.partial(_norm_mod_kernel, rows=rows, tm=tm, d=d, sh_col=sh_col, sc_col=sc_col),
        grid=(m // tm,),
        in_specs=[pl.BlockSpec((tm, d), lambda i: (i, 0)),
                  pl.BlockSpec((1, d), lambda i: (0, 0)),
                  pl.BlockSpec(mods.shape, lambda i: (0, 0))],
        out_specs=pl.BlockSpec((tm, d), lambda i: (i, 0)),
        out_shape=jax.ShapeDtypeStruct((m, d), out_dtype),
        compiler_params=_cparams(1),
        name="norm_mod",
    )(x2, g.reshape(1, d), mods)


def _final_norm_kernel(x_ref, g_ref, o_ref):
    o_ref[...] = _rms(x_ref[...]) * g_ref[...]


def _final_norm(x2, g):
    m, d = x2.shape
    tm = _pick(m, (512, 256, 128, 64))
    return pl.pallas_call(
        _final_norm_kernel,
        grid=(m // tm,),
        in_specs=[pl.BlockSpec((tm, d), lambda i: (i, 0)), pl.BlockSpec((1, d), lambda i: (0, 0))],
        out_specs=pl.BlockSpec((tm, d), lambda i: (i, 0)),
        out_shape=jax.ShapeDtypeStruct((m, d), F32),
        compiler_params=_cparams(1),
        name="final_norm",
    )(x2, g.reshape(1, d))


def _edge_masks(rows, i, tm):
    nsub = tm // rows.sub
    rid = lax.broadcasted_iota(jnp.int32, (tm, 1), 0)
    is_start = jnp.zeros((tm, 1), jnp.bool_)
    is_end = jnp.zeros((tm, 1), jnp.bool_)
    for s in range(nsub):
        _, st, en = rows.info(i * nsub + s)
        is_start = is_start | (rid == jnp.where(st, s * rows.sub, -1))
        is_end = is_end | (rid == jnp.where(en, (s + 1) * rows.sub - 1, -1))
    return rid, is_start, is_end


def _shifted(cur, prev_row, next_row, rid, is_start, is_end, tm):
    prev = jnp.where(rid == 0, prev_row, pltpu.roll(cur, 1, 0))
    nxt = jnp.where(rid == tm - 1, next_row, pltpu.roll(cur, tm - 1, 0))
    return jnp.where(is_start, 0.0, prev), jnp.where(is_end, 0.0, nxt)


def _rwkv_prep_kernel(x_ref, xp_ref, xn_ref, g_ref, m_ref, h_ref, xx_ref, *, rows, tm, d):
    i = pl.program_id(0)
    nsub = tm // rows.sub
    total_sub = rows.total // rows.sub

    def mod_rows(q):
        idx, _, _ = rows.info(q)
        return m_ref[pl.ds(idx, 1), 0:d], m_ref[pl.ds(idx, 1), d:2 * d]

    def h_of(x, q):
        sh, sc = mod_rows(q)
        return (_rms(x) * g_ref[...]) * (1.0 + sc) + sh

    for s in range(nsub):
        sl = slice(s * rows.sub, (s + 1) * rows.sub)
        h_ref[sl, :] = h_of(x_ref[sl, :], i * nsub + s)
    h = h_ref[...]
    hp = h_of(xp_ref[SUBLANES - 1:SUBLANES, :], jnp.maximum(i * nsub - 1, 0))
    hn = h_of(xn_ref[0:1, :], jnp.minimum((i + 1) * nsub, total_sub - 1))
    rid, is_start, is_end = _edge_masks(rows, i, tm)
    prev, nxt = _shifted(h, hp, hn, rid, is_start, is_end, tm)
    xx_ref[...] = 0.5 * (prev + nxt) - h


def _rwkv_prep(x2, g, mods, rows):
    m, d = x2.shape
    tm = max(_pick(m, (256, 128, 64)), rows.sub)
    r8 = tm // SUBLANES
    nblk8 = m // SUBLANES
    return pl.pallas_call(
        functools.partial(_rwkv_prep_kernel, rows=rows, tm=tm, d=d),
        grid=(m // tm,),
        in_specs=[pl.BlockSpec((tm, d), lambda i: (i, 0)),
                  pl.BlockSpec((SUBLANES, d), lambda i: (jnp.maximum(i * r8 - 1, 0), 0)),
                  pl.BlockSpec((SUBLANES, d), lambda i: (jnp.minimum((i + 1) * r8, nblk8 - 1), 0)),
                  pl.BlockSpec((1, d), lambda i: (0, 0)),
                  pl.BlockSpec(mods.shape, lambda i: (0, 0))],
        out_specs=[pl.BlockSpec((tm, d), lambda i: (i, 0)), pl.BlockSpec((tm, d), lambda i: (i, 0))],
        out_shape=[jax.ShapeDtypeStruct((m, d), F32), jax.ShapeDtypeStruct((m, d), F32)],
        compiler_params=_cparams(1),
        name="rwkv_prep",
    )(x2, x2, x2, g.reshape(1, d), mods)


def _mm_kernel(*refs, na, ne, no, prologue, epilogue, period):
    a_refs = refs[:na]
    w_ref = refs[na]
    e_refs = refs[na + 1:na + 1 + ne]
    o_refs = refs[na + 1 + ne:na + 1 + ne + no]
    i = pl.program_id(0)
    j = pl.program_id(1)
    if prologue is _first_input:
        lhs = a_refs[0][...]
    else:
        a_scr = refs[-1]

        @pl.when(j % period == 0)
        def _():
            a_scr[...] = prologue(a_refs, i, j // period)

        lhs = a_scr[...]
    acc = jnp.dot(lhs, w_ref[...], preferred_element_type=F32)
    epilogue(acc, e_refs, o_refs, i, j)


def _mm(name, a_arrays, a_specs, prologue, w, e_arrays, e_specs, epilogue, out_shapes, out_specs,
        m, k, n, tm, tn, period=None):
    nj = n // tn
    period = nj if period is None else period
    return pl.pallas_call(
        functools.partial(_mm_kernel, na=len(a_arrays), ne=len(e_arrays), no=len(out_shapes),
                          prologue=prologue, epilogue=epilogue, period=period),
        grid=(m // tm, nj),
        in_specs=list(a_specs) + [pl.BlockSpec((k, tn), lambda i, j: (0, j))] + list(e_specs),
        out_specs=list(out_specs),
        out_shape=list(out_shapes),
        scratch_shapes=[] if prologue is _first_input else [pltpu.VMEM((tm, k), BF16)],
        compiler_params=_cparams(2),
        name=name,
    )(*a_arrays, w, *e_arrays)


def _gated_residual_epilogue(rows, tm):
    nsub = tm // rows.sub

    def epi(acc, e_refs, o_refs, i, j):
        res_ref, m_ref = e_refs
        for s in range(nsub):
            idx, _, _ = rows.info(i * nsub + s)
            sl = slice(s * rows.sub, (s + 1) * rows.sub)
            o_refs[0][sl, :] = res_ref[sl, :] + m_ref[pl.ds(idx, 1), :] * acc[sl, :]
    return epi


def _store_epilogue(acc, e_refs, o_refs, i, j):
    o_refs[0][...] = acc.astype(o_refs[0].dtype)


def _first_input(a_refs, i, grp):
    return a_refs[0][...]


def _proj_residual(name, a_arrays, a_specs, prologue, w, res, mods, rows, gate_col, tm, tn):
    m, d = res.shape
    k = w.shape[0]
    gate_blk = gate_col * (d // tn)
    return _mm(name, a_arrays, a_specs, prologue, w,
               [res, mods],
               [pl.BlockSpec((tm, tn), lambda i, j: (i, j)),
                pl.BlockSpec((MOD_ROWS, tn), lambda i, j: (0, gate_blk + j))],
               _gated_residual_epilogue(rows, tm),
               [jax.ShapeDtypeStruct((m, d), F32)],
               [pl.BlockSpec((tm, tn), lambda i, j: (i, j))],
               m, k, d, tm, tn)[0]


def _softplus(z):
    return jnp.maximum(z, 0.0) + jnp.log(1.0 + jnp.exp(-jnp.abs(z)))


def _pack_kernel(x_ref, o_ref, z_ref, *, b, h, n, tiles):
    hp = LANES // n
    if b * h < LANES:
        z_ref[...] = jnp.zeros_like(z_ref)
    for bb in range(b):
        for c in range(h // hp):
            t = x_ref[bb, :, c * LANES:(c + 1) * LANES].T
            for hh in range(hp):
                lane = bb * h + c * hp + hh
                z_ref[lane * PITCH:lane * PITCH + n, :] = t[hh * n:(hh + 1) * n, :]
    for k in range(n):
        m = z_ref[pl.ds(k, LANES, stride=PITCH), :].T
        if tiles:
            o_ref[k // SUBLANES, :, k % SUBLANES, :] = m
        else:
            o_ref[k] = m


def _pack(x2, b, lt, h, n, g0, groups, tiles):
    d = h * n
    tt = LANES
    x3 = x2.reshape(b, lt, x2.shape[1])
    if tiles:
        oshape, oblk = (groups, n // SUBLANES, lt, SUBLANES, LANES), (None, n // SUBLANES, tt, SUBLANES, LANES)
        omap = lambda g, ti: (g, 0, ti, 0, 0)
    else:
        oshape, oblk = (groups, n, lt, LANES), (None, n, tt, LANES)
        omap = lambda g, ti: (g, 0, ti, 0)
    return pl.pallas_call(
        functools.partial(_pack_kernel, b=b, h=h, n=n, tiles=tiles),
        grid=(groups, lt // tt),
        in_specs=[pl.BlockSpec((b, tt, d), lambda g, ti: (0, ti, g0 + g))],
        out_specs=pl.BlockSpec(oblk, omap),
        out_shape=jax.ShapeDtypeStruct(oshape, F32),
        scratch_shapes=[pltpu.VMEM((LANES * PITCH, tt), F32)],
        compiler_params=_cparams(2),
        name="lane_pack",
    )(x3)


def _unpack_kernel(o_ref, x_ref, z_ref, *, b, h, n):
    hp = LANES // n
    for k in range(n):
        z_ref[pl.ds(k, LANES, stride=PITCH), :] = o_ref[k // SUBLANES, :, k % SUBLANES, :].T
    for bb in range(b):
        for c in range(h // hp):
            parts = [z_ref[(bb * h + c * hp + hh) * PITCH:(bb * h + c * hp + hh) * PITCH + n, :] for hh in range(hp)]
            x_ref[bb, :, c * LANES:(c + 1) * LANES] = jnp.concatenate(parts, axis=0).T


def _unpack(o5, b, lt, h, n):
    d = h * n
    tt = LANES
    return pl.pallas_call(
        functools.partial(_unpack_kernel, b=b, h=h, n=n),
        grid=(lt // tt,),
        in_specs=[pl.BlockSpec((n // SUBLANES, tt, SUBLANES, LANES), lambda ti: (0, ti, 0, 0))],
        out_specs=pl.BlockSpec((b, tt, d), lambda ti: (0, ti, 0)),
        out_shape=jax.ShapeDtypeStruct((b, lt, d), F32),
        scratch_shapes=[pltpu.VMEM((LANES * PITCH, tt), F32)],
        compiler_params=_cparams(1),
        name="lane_unpack",
    )(o5).reshape(b * lt, d)


def _scan_kernel(rkf, vf, df, af, rkr, vr, dr, ar, kk_ref, ka_ref, yf_ref, yr_ref, s_ref, vec_ref,
                 *, tb, n, vchunk):
    j = pl.program_id(0)

    @pl.when(j == 0)
    def _():
        s_ref[...] = jnp.zeros_like(s_ref)

    kkc = kk_ref[...][:, None, :]
    kac = ka_ref[...][:, None, :]
    nvg = n // SUBLANES
    streams = ((rkf, vf, df, af, yf_ref), (rkr, vr, dr, ar, yr_ref))

    for dirn, (rk_, _, _, a_, _) in enumerate(streams):
        kt = rk_[1]
        at = a_[...]
        kx = kt * kkc
        nrm = jnp.sqrt(jnp.sum(kx * kx, axis=0, keepdims=True))
        kkn = kx / jnp.maximum(nrm, 1e-12)
        vec_ref[dirn, 0] = -kkn
        vec_ref[dirn, 1] = kkn * at
        vec_ref[dirn, 2] = kt * (1.0 + (at - 1.0) * kac)

    def bc(row):
        return jnp.broadcast_to(row, (SUBLANES, LANES))

    def tile(vg):
        return slice(vg * SUBLANES, (vg + 1) * SUBLANES)

    def make_step(dirn, rk_, v_, d_, y_):
        def step(i, sa):
            tt = i if dirn == 0 else tb - 1 - i
            tnext = jnp.minimum(tt + 1, tb - 1) if dirn == 0 else jnp.maximum(tt - 1, 0)
            sa_next = [None] * nvg
            for c in range(nvg // vchunk):
                vgs = [c * vchunk + q for q in range(vchunk)]
                vv = [v_[vg, tt] for vg in vgs]
                yy = [jnp.zeros((SUBLANES, LANES), F32) for _ in vgs]
                sn = [jnp.zeros((SUBLANES, LANES), F32) for _ in vgs]
                for kk in range(n):
                    wb = bc(d_[kk, pl.ds(tt, 1), :])
                    bb = bc(vec_ref[dirn, 1, kk, pl.ds(tt, 1), :])
                    kb = bc(vec_ref[dirn, 2, kk, pl.ds(tt, 1), :])
                    rb = bc(rk_[0, kk, pl.ds(tt, 1), :])
                    an = bc(vec_ref[dirn, 0, kk, pl.ds(tnext, 1), :])
                    for q, vg in enumerate(vgs):
                        snew = s_ref[dirn, kk, tile(vg), :] * wb + sa[vg] * bb + vv[q] * kb
                        s_ref[dirn, kk, tile(vg), :] = snew
                        yy[q] = yy[q] + snew * rb
                        sn[q] = sn[q] + snew * an
                for q, vg in enumerate(vgs):
                    y_[vg, tt] = yy[q]
                    sa_next[vg] = sn[q]
            return tuple(sa_next)
        return step

    for dirn, (rk_, v_, d_, _, y_) in enumerate(streams):
        t0 = 0 if dirn == 0 else tb - 1
        sa0 = [jnp.zeros((SUBLANES, LANES), F32) for _ in range(nvg)]
        for kk in range(n):
            a0 = bc(vec_ref[dirn, 0, kk, t0:t0 + 1, :])
            for vg in range(nvg):
                sa0[vg] = sa0[vg] + s_ref[dirn, kk, tile(vg), :] * a0
        lax.fori_loop(0, tb, make_step(dirn, rk_, v_, d_, y_), tuple(sa0))


def _rwkv_scan(rk, v, dec, icl, kk_c, ka_c, n_ctx, tb):
    _, n, lt, ln = rk.shape
    nvg = n // SUBLANES
    nb = lt // tb
    nc = n_ctx // tb

    def rev(j):
        return jnp.where(j < nc, nc - 1 - j, nb - 1 - (j - nc))

    def specs(tmap, dirn):
        return [pl.BlockSpec((2, n, tb, ln), lambda j: (0, 0, tmap(j), 0)),
                pl.BlockSpec((None, nvg, tb, SUBLANES, ln), lambda j: (0, 0, tmap(j), 0, 0)),
                pl.BlockSpec((None, n, tb, ln), lambda j: (dirn, 0, tmap(j), 0)),
                pl.BlockSpec((None, n, tb, ln), lambda j: (dirn, 0, tmap(j), 0))]

    ident = lambda j: j
    const = pl.BlockSpec((n, ln), lambda j: (0, 0))
    yshape = jax.ShapeDtypeStruct((nvg, lt, SUBLANES, ln), F32)
    return pl.pallas_call(
        functools.partial(_scan_kernel, tb=tb, n=n, vchunk=min(8, nvg)),
        grid=(nb,),
        in_specs=specs(ident, 0) + specs(rev, 1) + [const, const],
        out_specs=[pl.BlockSpec((nvg, tb, SUBLANES, ln), lambda j: (0, j, 0, 0)),
                   pl.BlockSpec((nvg, tb, SUBLANES, ln), lambda j: (0, rev(j), 0, 0))],
        out_shape=[yshape, yshape],
        scratch_shapes=[pltpu.VMEM((2, n, n, ln), F32), pltpu.VMEM((2, 3, n, tb, ln), F32)],
        compiler_params=_cparams(1),
        name="rwkv_scan",
    )(rk, v, dec, icl, rk, v, dec, icl, kk_c, ka_c)


def _readout_kernel(rk_ref, v_ref, a_ref, yf_ref, yr_ref, ka_ref, rkc_ref, gw_ref, gb_ref, o_ref, *, n):
    y = yf_ref[...] + yr_ref[...]
    mean = jnp.sum(jnp.sum(y, axis=0, keepdims=True), axis=2, keepdims=True) * (1.0 / n)
    yc = y - mean
    var = jnp.sum(jnp.sum(yc * yc, axis=0, keepdims=True), axis=2, keepdims=True) * (1.0 / n)
    yn = (yc * lax.rsqrt(var + GN_EPS)) * gw_ref[...][:, None] + gb_ref[...][:, None]
    r = rk_ref[0]
    k = rk_ref[1]
    kac = ka_ref[...][:, None, :]
    rrk = r * rkc_ref[...][:, None, :]
    kd0 = k * (1.0 + (a_ref[0] - 1.0) * kac)
    kd1 = k * (1.0 + (a_ref[1] - 1.0) * kac)
    bonus = jnp.sum(rrk * kd0, axis=0) + jnp.sum(rrk * kd1, axis=0)
    o_ref[...] = yn + bonus[None, :, None, :] * v_ref[...]


def _rwkv_readout(rk, v, icl, yf, yr, ka_c, rk_c, gw_t, gb_t, tb):
    _, n, lt, ln = rk.shape
    nvg = n // SUBLANES
    tile = pl.BlockSpec((nvg, tb, SUBLANES, ln), lambda j: (0, j, 0, 0))
    chan2 = pl.BlockSpec((2, n, tb, ln), lambda j: (0, 0, j, 0))
    const = pl.BlockSpec((n, ln), lambda j: (0, 0))
    const_t = pl.BlockSpec((nvg, SUBLANES, ln), lambda j: (0, 0, 0))
    return pl.pallas_call(
        functools.partial(_readout_kernel, n=n),
        grid=(lt // tb,),
        in_specs=[chan2, pl.BlockSpec((None, nvg, tb, SUBLANES, ln), lambda j: (0, 0, j, 0, 0)), chan2, tile, tile,
                  const, const, const_t, const_t],
        out_specs=tile,
        out_shape=jax.ShapeDtypeStruct((nvg, lt, SUBLANES, ln), F32),
        compiler_params=_cparams(1),
        name="rwkv_readout",
    )(rk, v, icl, yf, yr, ka_c, rk_c, gw_t, gb_t)


NT_DIMS = (((1,), (1,)), ((), ()))


def _split_dot(x, ones, terms, ones_left=False):
    acc = None
    rem = x
    for _ in range(terms):
        piece = rem.astype(BF16)
        part = (jnp.dot(ones, piece, preferred_element_type=F32) if ones_left
                else jnp.dot(piece, ones, preferred_element_type=F32))
        acc = part if acc is None else acc + part
        rem = rem - piece.astype(F32)
    return acc
TN_DIMS = (((0,), (0,)), ((), ()))


def _chunk_kernel(rf, kf, vf, lwf, icf, rr, kr, vr, lwr, icr, kk_ref, ka_ref, yf_ref, yr_ref, s_ref, *, c, n, hb):
    j = pl.program_id(2)

    @pl.when(j == 0)
    def _():
        s_ref[...] = jnp.zeros_like(s_ref)

    hp = LANES // n
    trow = lax.broadcasted_iota(jnp.int32, (c, LANES), 0)
    lane = lax.broadcasted_iota(jnp.int32, (c, LANES), 1)
    scol = lane & (c - 1)
    row = lax.broadcasted_iota(jnp.int32, (c, c), 0)
    col = lax.broadcasted_iota(jnp.int32, (c, c), 1)
    lane_head = lane // n
    seg_b = (lax.broadcasted_iota(jnp.int32, (LANES, LANES), 0) // n
             == lax.broadcasted_iota(jnp.int32, (LANES, LANES), 1) // n)
    seg = seg_b.astype(F32).astype(BF16)
    doublings = int(math.log2(c))

    def bdiag(x):
        return jnp.concatenate([jnp.where(lane_head == h, x, jnp.zeros_like(x)) for h in range(hp)], axis=0)

    streams = ((rf, kf, vf, lwf, icf, yf_ref, col <= row, scol <= trow, scol < trow, c - 1),
               (rr, kr, vr, lwr, icr, yr_ref, col >= row, scol >= trow, scol > trow, 0))
    chains = []
    for dirn, (r_, k_, v_, lw_, ic_, y_, incl, incl_p, strict_p, last) in enumerate(streams):
        tri = incl.astype(F32).astype(BF16)
        for p in range(hb // hp):
            sl = slice(p * LANES, (p + 1) * LANES)
            r, k, v, lw, ic = r_[:, sl], k_[:, sl], v_[:, sl], lw_[:, sl], ic_[:, sl]
            kx = k * kk_ref[:, sl]
            nrm = jnp.sqrt(_split_dot(kx * kx, seg, 3))
            kkn = kx / jnp.maximum(nrm, 1e-12)
            g = _split_dot(lw, tri, 3, ones_left=True)
            eig = jnp.exp(-g)
            kq = ((k * (1.0 + (ic - 1.0) * ka_ref[:, sl])) * eig).astype(BF16)
            bq = ((kkn * ic) * eig).astype(BF16)
            chains.append(dict(
                dirn=dirn, p=p, sl=sl, y_ref=y_, incl_p=incl_p, strict_p=strict_p, eg=jnp.exp(g[last:last + 1, :]),
                vb=v.astype(BF16), qm=jnp.concatenate([kq, bq], axis=0),
                pm=jnp.concatenate([r * jnp.exp(g), -kkn * jnp.exp(g - lw)], axis=0).astype(BF16),
                qbd=jnp.concatenate([bdiag(kq), bdiag(bq)], axis=0)))
    for ch in chains:
        gp = lax.dot_general(ch["pm"], ch["qbd"], NT_DIMS, preferred_element_type=F32)
        ch["a_r"] = jnp.concatenate([jnp.where(ch["incl_p"], gp[:c, :LANES], 0.0),
                                     jnp.where(ch["strict_p"], gp[c:, :LANES], 0.0)], axis=0).astype(BF16)
        ch["a_rb"] = jnp.where(ch["incl_p"], gp[:c, LANES:], 0.0).astype(BF16)
        ch["m"] = jnp.where(ch["strict_p"], gp[c:, LANES:], 0.0)
    for ch in chains:
        ch["s0"] = s_ref[ch["dirn"], ch["p"]]
        ch["ps"] = lax.dot_general(ch["pm"], ch["s0"].astype(BF16), NT_DIMS, preferred_element_type=F32)
    for ch in chains:
        t = ch["ps"] + jnp.dot(ch["a_r"], bdiag(ch["vb"]), preferred_element_type=F32)
        ch["y0"], ch["x"] = t[:c], t[c:]
    for it in range(doublings):
        for ch in chains:
            m, x = ch["m"], ch["x"]
            mb, xb = m.astype(BF16), x.astype(BF16)
            xl = (x - xb.astype(F32)).astype(BF16)
            ch["x"] = x + (jnp.dot(mb, bdiag(xb), preferred_element_type=F32)
                           + jnp.dot(mb, bdiag(xl), preferred_element_type=F32))
            if it + 1 < doublings:
                ch["m"] = jnp.dot(mb, bdiag(mb), preferred_element_type=F32)
    for ch in chains:
        ub = ch["x"].astype(BF16)
        ch["y_ref"][:, ch["sl"]] = ch["y0"] + jnp.dot(ch["a_rb"], bdiag(ub), preferred_element_type=F32)
        vu = jnp.concatenate([ch["vb"], ub], axis=0)
        upd = lax.dot_general(vu, ch["qm"], TN_DIMS, preferred_element_type=F32)
        s_ref[ch["dirn"], ch["p"]] = (ch["s0"] + jnp.where(seg_b, upd, 0.0)) * ch["eg"]


def _rwkv_chunked(rkv, logw, icl, k_k, k_a, b, lt, n_ctx, h, n):
    d = h * n
    c = CHUNK
    assert c == n and LANES % n == 0, "slab layout: chunk length equals the head size"
    hb = min(CHUNK_HEADS, h)
    nch, nc, ng, w = lt // c, n_ctx // c, h // hb, hb * n

    def rev(j):
        return jnp.where(j < nc, nc - 1 - j, nch - 1 - (j - nc))

    def spec(tmap, colgrp):
        return pl.BlockSpec((c, w), lambda bb, g, j: (bb * nch + tmap(j), colgrp * ng + g))

    ident = lambda j: j
    const = pl.BlockSpec((1, w), lambda bb, g, j: (0, g))
    return pl.pallas_call(
        functools.partial(_chunk_kernel, c=c, n=n, hb=hb),
        grid=(b, ng, nch),
        in_specs=[spec(ident, 0), spec(ident, 1), spec(ident, 2), spec(ident, 0), spec(ident, 0),
                  spec(rev, 0), spec(rev, 1), spec(rev, 2), spec(rev, 1), spec(rev, 1), const, const],
        out_specs=[spec(ident, 0), spec(rev, 0)],
        out_shape=[jax.ShapeDtypeStruct((b * lt, d), F32)] * 2,
        scratch_shapes=[pltpu.VMEM((2, hb * n // LANES, LANES, LANES), F32)],
        compiler_params=_cparams(3),
        name="rwkv_chunk",
    )(rkv, rkv, rkv, logw, icl, rkv, rkv, rkv, logw, icl, k_k.reshape(1, d), k_a.reshape(1, d))


def _readout_tok_kernel(yf_ref, yr_ref, r_ref, k_ref, v_ref, a0_ref, a1_ref, g_ref, ka_ref, rk_ref, gw_ref, gb_ref,
                        o_ref, *, n, d):
    seg = (lax.broadcasted_iota(jnp.int32, (LANES, LANES), 0) // n
           == lax.broadcasted_iota(jnp.int32, (LANES, LANES), 1) // n).astype(F32).astype(BF16)

    def seg_sum(t):
        return _split_dot(t, seg, 2)

    for p in range(d // LANES):
        sl = slice(p * LANES, (p + 1) * LANES)
        y = yf_ref[:, sl] + yr_ref[:, sl]
        yc = y - seg_sum(y) * (1.0 / n)
        var = seg_sum(yc * yc) * (1.0 / n)
        yn = (yc * lax.rsqrt(var + GN_EPS)) * gw_ref[:, sl] + gb_ref[:, sl]
        k = k_ref[:, sl]
        ka = ka_ref[:, sl]
        kd = k * (1.0 + (a0_ref[:, sl] - 1.0) * ka) + k * (1.0 + (a1_ref[:, sl] - 1.0) * ka)
        bonus = seg_sum(r_ref[:, sl] * rk_ref[:, sl] * kd)
        o_ref[:, sl] = ((yn + bonus * v_ref[:, sl]) * g_ref[:, sl]).astype(o_ref.dtype)


def _rwkv_readout_tok(yf, yr, rkv, icl, gate, k_a, r_k, gn_w, gn_b, n):
    m, d = yf.shape
    tm = _pick(m, (128, 64))
    blk = lambda cg: pl.BlockSpec((tm, d), lambda i: (i, cg))
    const = pl.BlockSpec((1, d), lambda i: (0, 0))
    return pl.pallas_call(
        functools.partial(_readout_tok_kernel, n=n, d=d),
        grid=(m // tm,),
        in_specs=[blk(0), blk(0), blk(0), blk(1), blk(2), blk(0), blk(1), blk(0)] + [const] * 4,
        out_specs=blk(0),
        out_shape=jax.ShapeDtypeStruct((m, d), BF16),
        compiler_params=_cparams(1),
        name="rwkv_readout",
    )(yf, yr, rkv, rkv, rkv, icl, icl, gate, k_a.reshape(1, d), r_k.reshape(1, d), gn_w.reshape(1, d), gn_b.reshape(1, d))


def _ffn_up_kernel(h_ref, hp_ref, hn_ref, wg_ref, wv_ref, cw_ref, cb_ref, o_ref, *, rows, tm):
    i = pl.program_id(0)
    h = h_ref[...]
    wg = wg_ref[...]
    gate = jnp.dot(h, wg, preferred_element_type=F32)
    val = jnp.dot(h, wv_ref[...], preferred_element_type=F32)
    gp = jnp.dot(hp_ref[...], wg, preferred_element_type=F32)[BF16_ROWS - 1:BF16_ROWS, :]
    gn = jnp.dot(hn_ref[...], wg, preferred_element_type=F32)[0:1, :]
    rid, is_start, is_end = _edge_masks(rows, i, tm)
    prev, nxt = _shifted(gate, gp, gn, rid, is_start, is_end, tm)
    conv = prev * cw_ref[0:1, :] + gate * cw_ref[1:2, :] + nxt * cw_ref[2:3, :] + cb_ref[...]
    o_ref[...] = (_silu(conv) * val).astype(o_ref.dtype)


def _ffn_up(hf, wup, conv_w, conv_b, rows, tm, tn):
    m, d = hf.shape
    f = wup.shape[1] // 2
    r8 = tm // BF16_ROWS
    nblk8 = m // BF16_ROWS
    njf = f // tn
    cw = jnp.zeros((SUBLANES, f), F32).at[:conv_w.shape[0]].set(conv_w)
    return pl.pallas_call(
        functools.partial(_ffn_up_kernel, rows=rows, tm=tm),
        grid=(m // tm, njf),
        in_specs=[pl.BlockSpec((tm, d), lambda i, j: (i, 0)),
                  pl.BlockSpec((BF16_ROWS, d), lambda i, j: (jnp.maximum(i * r8 - 1, 0), 0)),
                  pl.BlockSpec((BF16_ROWS, d), lambda i, j: (jnp.minimum((i + 1) * r8, nblk8 - 1), 0)),
                  pl.BlockSpec((d, tn), lambda i, j: (0, j)),
                  pl.BlockSpec((d, tn), lambda i, j: (0, j + njf)),
                  pl.BlockSpec((SUBLANES, tn), lambda i, j: (0, j)),
                  pl.BlockSpec((1, tn), lambda i, j: (0, j))],
        out_specs=pl.BlockSpec((tm, tn), lambda i, j: (i, j)),
        out_shape=jax.ShapeDtypeStruct((m, f), BF16),
        compiler_params=_cparams(2),
        name="ffn_up",
    )(hf, hf, hf, wup, wup, cw, conv_b.reshape(1, f))


def _conv_ffn(x2, g, mods, rows, wup, conv_w, conv_b, wdown, tm):
    m, d = x2.shape
    f = wdown.shape[0]
    hf = _norm_mod(x2, g, mods, rows, 3, 4, BF16)
    act = _ffn_up(hf, wup.astype(BF16), conv_w, conv_b, rows, tm, _pick(f, (512, 256, 128)))
    tmd = _pick(m, tuple(t for t in (1024, 512, 256, 128, 64) if t <= tm and t % rows.sub == 0))
    return _proj_residual("ffn_down", [act], [pl.BlockSpec((tmd, f), lambda i, j: (i, 0))],
                          _first_input, wdown.astype(BF16), x2, mods, rows, 5,
                          tmd, _pick(d, (512, 256, 128)))


def _rope(x, cos, sin_signed):
    lane = lax.broadcasted_iota(jnp.int32, x.shape, 1)
    first = (lane & 31) < 16
    swapped = jnp.where(first, pltpu.roll(x, LANES - 16, 1), pltpu.roll(x, 16, 1))
    return x * cos + swapped * sin_signed


def _kv_up_kernel(c_ref, wk_ref, wv_ref, kr_ref, ko_ref, vo_ref, *, hpt):
    c = c_ref[...]
    kn = jnp.dot(c, wk_ref[...], preferred_element_type=F32).astype(BF16)
    vo_ref[...] = jnp.dot(c, wv_ref[...], preferred_element_type=F32).astype(BF16)
    kr = kr_ref[...]
    for h in range(hpt):
        ko_ref[:, h * 2 * LANES:h * 2 * LANES + QK_NOPE] = kn[:, h * QK_NOPE:(h + 1) * QK_NOPE]
        ko_ref[:, h * 2 * LANES + QK_NOPE:(h + 1) * 2 * LANES] = kr


def _attn_kernel(q_ref, k_ref, v_ref, o_ref, *, hpa, hq):
    for h in range(hpa):
        q = q_ref[:, h * hq:(h + 1) * hq]
        k = k_ref[:, h * hq:(h + 1) * hq]
        s = lax.dot_general(q, k, (((1,), (1,)), ((), ())), preferred_element_type=F32)
        p = jnp.exp2(s - jnp.max(s, axis=-1, keepdims=True))
        denom = jnp.sum(p, axis=-1, keepdims=True)
        o = jnp.dot(p.astype(BF16), v_ref[:, h * V_HEAD:(h + 1) * V_HEAD], preferred_element_type=F32)
        o_ref[:, h * V_HEAD:(h + 1) * V_HEAD] = (o / denom).astype(o_ref.dtype)


def _rwkv_layer(xa, mods, norm_g, rows, p, tm):
    (mu, wr, wk, wv, wo, w0, w1, w2, a0, a1, a2, g1, g2, k_k, k_a, r_k, gn_w, gn_b) = p
    m, d = xa.shape
    h_heads, n = r_k.shape
    b, lt = rows.batch, rows.per_batch
    glora, dlora, alora = g1.shape[1], w1.shape[2], a1.shape[2]
    gw = _round_up(max(glora, 2 * dlora, 2 * alora), LANES)

    h, xx = _rwkv_prep(xa, norm_g, mods, rows)

    a_specs = [pl.BlockSpec((tm, d), lambda i, j: (i, 0)), pl.BlockSpec((tm, d), lambda i, j: (i, 0)),
               pl.BlockSpec((MOD_ROWS, d), lambda i, j: (0, 0))]

    def mix(a, i, grp):
        return (a[0][...] + a[1][...] * a[2][pl.ds(grp, 1), :]).astype(BF16)

    def pad_rows(t):
        return jnp.zeros((MOD_ROWS, d), F32).at[:t.shape[0]].set(t)

    tn = _pick(d, (512, 256, 128))
    w_rkv = jnp.concatenate([wr, wk, wv], axis=1).astype(BF16)
    rkv = _mm("rwkv_rkv", [h, xx, pad_rows(mu[jnp.array([0, 2, 3])])], a_specs, mix, w_rkv, [], [],
              _store_epilogue,
              [jax.ShapeDtypeStruct((m, 3 * d), F32)], [pl.BlockSpec((tm, tn), lambda i, j: (i, j))],
              m, d, 3 * d, tm, tn, period=d // tn)[0]

    def pad_cols(t):
        return jnp.pad(t, ((0, 0), (0, gw - t.shape[1])))

    w_lora = jnp.concatenate([pad_cols(g1), pad_cols(jnp.concatenate([w1[0], w1[1]], axis=1)),
                              pad_cols(jnp.concatenate([a1[0], a1[1]], axis=1))], axis=1).astype(BF16)

    def lora_act(acc, e, o, i, j):
        o[0][...] = jnp.where(j == 0, jax.nn.sigmoid(acc), jnp.where(j == 1, jnp.tanh(acc), acc)).astype(BF16)

    hid = _mm("rwkv_lora1", [h, xx, pad_rows(mu[jnp.array([5, 1, 4])])], a_specs, mix, w_lora, [], [], lora_act,
              [jax.ShapeDtypeStruct((m, 3 * gw), BF16)], [pl.BlockSpec((tm, gw), lambda i, j: (i, j))],
              m, d, 3 * gw, tm, gw, period=1)[0]

    def hid_spec(grp):
        return [pl.BlockSpec((tm, gw), lambda i, j: (i, grp))]

    ident = _first_input
    g2p = jnp.zeros((gw, d), F32).at[:glora].set(g2).astype(BF16)
    gate = _mm("rwkv_gate", [hid], hid_spec(0), ident, g2p, [], [],
               _store_epilogue,
               [jax.ShapeDtypeStruct((m, d), F32)], [pl.BlockSpec((tm, tn), lambda i, j: (i, j))],
               m, gw, d, tm, tn)[0]

    def two_dir(t2, lora):
        z = jnp.zeros((gw, 2 * d), F32)
        z = z.at[:lora, :d].set(t2[0]).at[lora:2 * lora, d:].set(t2[1])
        return z.astype(BF16)

    bias_spec = [pl.BlockSpec((1, tn), lambda i, j: (0, j))]

    def decay_epi(acc, e, o, i, j):
        w_log = -_softplus(-(e[0][...] + acc)) - 0.5
        o[0][...] = -jnp.exp(w_log)

    logw = _mm("rwkv_decay", [hid], hid_spec(1), ident, two_dir(w2, dlora), [w0.reshape(1, 2 * d)], bias_spec,
               decay_epi, [jax.ShapeDtypeStruct((m, 2 * d), F32)], [pl.BlockSpec((tm, tn), lambda i, j: (i, j))],
               m, gw, 2 * d, tm, tn)[0]

    def iclr_epi(acc, e, o, i, j):
        o[0][...] = jax.nn.sigmoid(e[0][...] + acc)

    icl = _mm("rwkv_iclr", [hid], hid_spec(2), ident, two_dir(a2, alora), [a0.reshape(1, 2 * d)], bias_spec,
              iclr_epi, [jax.ShapeDtypeStruct((m, 2 * d), F32)], [pl.BlockSpec((tm, tn), lambda i, j: (i, j))],
              m, gw, 2 * d, tm, tn)[0]

    yf, yr = _rwkv_chunked(rkv, logw, icl, k_k, k_a, b, lt, rows.n_ctx, h_heads, n)
    og = _rwkv_readout_tok(yf, yr, rkv, icl, gate, k_a, r_k.reshape(d), gn_w, gn_b, n)
    return _proj_residual("rwkv_out", [og], [pl.BlockSpec((tm, d), lambda i, j: (i, 0))], _first_input,
                          wo.astype(BF16), xa, mods, rows, 2, tm, tn)


def _mla_layer(xa, xl, mods, norm_g, rows_u, rows_l, p, tm_u, tm_l):
    wdown, qnorm, kvnorm, wuq, wukv, wo = p
    m, d = xa.shape
    b, lt, lc, ll = rows_u.batch, rows_u.per_batch, rows_u.n_ctx, rows_u.n_lat
    ql, kvl = qnorm.shape[0], kvnorm.shape[0]
    heads = wuq.shape[1] // (QK_NOPE + QK_ROPE)
    hq = 2 * LANES

    npairs = QK_ROPE // 4
    pos = jnp.arange(ll)
    inv_freq = jnp.float32(ROPE_THETA) ** (-jnp.arange(npairs, dtype=F32) / npairs)
    ang_r = (pos // GRID_W).astype(F32)[:, None] * inv_freq
    ang_c = (pos % GRID_W).astype(F32)[:, None] * inv_freq
    cos64 = jnp.concatenate([jnp.cos(ang_r)] * 2 + [jnp.cos(ang_c)] * 2, axis=1)
    sin64 = jnp.concatenate([-jnp.sin(ang_r), jnp.sin(ang_r), -jnp.sin(ang_c), jnp.sin(ang_c)], axis=1)
    padl = ((lc, 0), (0, LANES - QK_ROPE))
    cos_t = jnp.tile(jnp.pad(cos64, padl, constant_values=1.0), (b, 1))
    sin_t = jnp.tile(jnp.pad(sin64, padl), (b, 1))

    h1 = _norm_mod(xa, norm_g, mods, rows_u, 0, 1, BF16)

    nd = _round_up(ql + kvl + QK_ROPE, LANES)
    wd = jnp.pad(wdown, ((0, 0), (0, nd - wdown.shape[1]))).astype(BF16)

    def down_epi(acc, e, o, i, j):
        qn_ref, kvn_ref, cos_ref, sin_ref = e
        o[0][...] = (_rms(acc[:, :ql]) * qn_ref[...]).astype(BF16)
        o[1][...] = (_rms(acc[:, ql:ql + kvl]) * kvn_ref[...]).astype(BF16)
        o[2][...] = _rope(acc[:, ql + kvl:], cos_ref[...], sin_ref[...]).astype(BF16)

    tab_spec = pl.BlockSpec((tm_u, LANES), lambda i, j: (i, 0))
    cq, ckv, krope = _mm(
        "mla_down", [h1], [pl.BlockSpec((tm_u, d), lambda i, j: (i, 0))], _first_input, wd,
        [qnorm.reshape(1, ql), kvnorm.reshape(1, kvl), cos_t, sin_t],
        [pl.BlockSpec((1, ql), lambda i, j: (0, 0)), pl.BlockSpec((1, kvl), lambda i, j: (0, 0)), tab_spec, tab_spec],
        down_epi,
        [jax.ShapeDtypeStruct((m, ql), BF16), jax.ShapeDtypeStruct((m, kvl), BF16),
         jax.ShapeDtypeStruct((m, LANES), BF16)],
        [pl.BlockSpec((tm_u, ql), lambda i, j: (i, 0)), pl.BlockSpec((tm_u, kvl), lambda i, j: (i, 0)),
         pl.BlockSpec((tm_u, LANES), lambda i, j: (i, 0))],
        m, d, nd, tm_u, nd)

    wq = wuq.reshape(ql, heads, QK_NOPE + QK_ROPE)
    wq = jnp.pad(wq, ((0, 0), (0, 0), (0, hq - QK_NOPE - QK_ROPE))).reshape(ql, heads * hq).astype(BF16)
    hpt = _pick(heads, (4, 2, 1))

    qscale = (QK_NOPE + QK_ROPE) ** -0.5 * math.log2(math.e)

    def q_epi(acc, e, o, i, j):
        cos_ref, sin_ref = e
        acc = acc * qscale
        for hh in range(hpt):
            lo = hh * hq
            o[0][:, lo:lo + QK_NOPE] = acc[:, lo:lo + QK_NOPE].astype(BF16)
            o[0][:, lo + QK_NOPE:lo + hq] = _rope(acc[:, lo + QK_NOPE:lo + hq], cos_ref[...], sin_ref[...]).astype(BF16)

    q = _mm("mla_q_up", [cq], [pl.BlockSpec((tm_u, ql), lambda i, j: (i, 0))], _first_input, wq,
            [cos_t, sin_t], [tab_spec, tab_spec], q_epi,
            [jax.ShapeDtypeStruct((m, heads * hq), BF16)], [pl.BlockSpec((tm_u, hpt * hq), lambda i, j: (i, j))],
            m, ql, heads * hq, tm_u, hpt * hq)[0]

    wkv = wukv.reshape(kvl, heads, QK_NOPE + V_HEAD)
    wk = wkv[:, :, :QK_NOPE].reshape(kvl, heads * QK_NOPE).astype(BF16)
    wv = wkv[:, :, QK_NOPE:].reshape(kvl, heads * V_HEAD).astype(BF16)
    kcat, vals = pl.pallas_call(
        functools.partial(_kv_up_kernel, hpt=hpt),
        grid=(m // tm_u, heads // hpt),
        in_specs=[pl.BlockSpec((tm_u, kvl), lambda i, j: (i, 0)),
                  pl.BlockSpec((kvl, hpt * QK_NOPE), lambda i, j: (0, j)),
                  pl.BlockSpec((kvl, hpt * V_HEAD), lambda i, j: (0, j)),
                  pl.BlockSpec((tm_u, LANES), lambda i, j: (i, 0))],
        out_specs=[pl.BlockSpec((tm_u, hpt * hq), lambda i, j: (i, j)),
                   pl.BlockSpec((tm_u, hpt * V_HEAD), lambda i, j: (i, j))],
        out_shape=[jax.ShapeDtypeStruct((m, heads * hq), BF16), jax.ShapeDtypeStruct((m, heads * V_HEAD), BF16)],
        compiler_params=_cparams(2),
        name="mla_kv_up",
    )(ckv, wk, wv, krope)

    tq = _pick(math.gcd(lc, ll), (256, 128, 64))
    hpa = _pick(heads, (2, 1))
    o = pl.pallas_call(
        functools.partial(_attn_kernel, hpa=hpa, hq=hq),
        grid=(b, heads // hpa, ll // tq),
        in_specs=[pl.BlockSpec((None, tq, hpa * hq), lambda bb, hh, i: (bb, lc // tq + i, hh)),
                  pl.BlockSpec((None, lt, hpa * hq), lambda bb, hh, i: (bb, 0, hh)),
                  pl.BlockSpec((None, lt, hpa * V_HEAD), lambda bb, hh, i: (bb, 0, hh))],
        out_specs=pl.BlockSpec((None, tq, hpa * V_HEAD), lambda bb, hh, i: (bb, i, hh)),
        out_shape=jax.ShapeDtypeStruct((b, ll, heads * V_HEAD), BF16),
        compiler_params=_cparams(3),
        name="mla_attn",
    )(q.reshape(b, lt, heads * hq), kcat.reshape(b, lt, heads * hq), vals.reshape(b, lt, heads * V_HEAD))

    ko = heads * V_HEAD
    return _proj_residual("mla_out", [o.reshape(b * ll, ko)], [pl.BlockSpec((tm_l, ko), lambda i, j: (i, 0))],
                          _first_input, wo.astype(BF16), xl, mods, rows_l, 2,
                          tm_l, _pick(d, (512, 256, 128)))


def kernel(x, c, ctx, c_ctx, ada_w, ada_b, norm_g, final_g, rk_mu, rk_wr, rk_wk, rk_wv, rk_wo, rk_w0, rk_w1, rk_w2, rk_a0, rk_a1, rk_a2, rk_g1, rk_g2, rk_kk, rk_ka, rk_rk, rk_gn_w, rk_gn_b, ml_wdown, ml_qnorm, ml_kvnorm, ml_wuq, ml_wukv, ml_wo, ff_wup, ff_conv, ff_convb, ff_wdown):
    b, ll, d = x.shape
    lc = ctx.shape[1]
    depth = ada_w.shape[0]
    heads_r = rk_rk.shape[1]
    assert depth == 2 and rk_mu.shape[0] == 1 and ml_wdown.shape[0] == 1, "one RWKV layer then one MLA layer"
    assert b + 1 <= MOD_ROWS and b * heads_r <= LANES
    sub = _pick(math.gcd(lc, ll), (256, 128, 64, 32, 16, 8))
    rows_u = _Rows(b, lc, ll, sub)
    rows_l = _Rows(b, 0, ll, sub)
    tm_u = _pick(rows_u.total, tuple(t for t in (1024, 512, 256, 128, 64) if t % sub == 0))
    tm_l = _pick(rows_l.total, tuple(t for t in (1024, 512, 256, 128, 64) if t % sub == 0))

    cvec = jnp.zeros((MOD_ROWS, d), F32).at[:b].set(c).at[b].set(c_ctx)
    mods = _ada(cvec, ada_w, ada_b)

    xa = jnp.concatenate([ctx, x], axis=1).reshape(rows_u.total, d)

    p0 = (rk_mu[0], rk_wr[0], rk_wk[0], rk_wv[0], rk_wo[0], rk_w0[0], rk_w1[0], rk_w2[0], rk_a0[0], rk_a1[0],
          rk_a2[0], rk_g1[0], rk_g2[0], rk_kk[0], rk_ka[0], rk_rk[0], rk_gn_w[0], rk_gn_b[0])
    xa = _rwkv_layer(xa, mods[0], norm_g[0, 0], rows_u, p0, tm_u)
    xa = _conv_ffn(xa, norm_g[0, 1], mods[0], rows_u, ff_wup[0], ff_conv[0], ff_convb[0], ff_wdown[0], tm_u)

    xl = xa.reshape(b, rows_u.per_batch, d)[:, lc:].reshape(rows_l.total, d)
    p1 = (ml_wdown[0], ml_qnorm[0], ml_kvnorm[0], ml_wuq[0], ml_wukv[0], ml_wo[0])
    xl = _mla_layer(xa, xl, mods[1], norm_g[1, 0], rows_u, rows_l, p1, tm_u, tm_l)
    xl = _conv_ffn(xl, norm_g[1, 1], mods[1], rows_l, ff_wup[1], ff_conv[1], ff_convb[1], ff_wdown[1], tm_l)
    return _final_norm(xl, final_g).reshape(b, ll, d)
```

```python
import functools
import math

import jax
import jax.numpy as jnp
from jax import lax
from jax.experimental import pallas as pl
from jax.experimental.pallas import tpu as pltpu

F32 = jnp.float32
BF16 = jnp.bfloat16

RMS_EPS = 1e-6
GN_EPS = 64e-5
QK_NOPE = 128
QK_ROPE = 64
V_HEAD = 128
GRID_W = 64
ROPE_THETA = 10000.0
LANES = 128
SUBLANES = 8
BF16_ROWS = 16
CHUNK = 64
CHUNK_HEADS = 32
CHUNK_SLAB = 128
VMEM_LIMIT = 56 * 1024 * 1024
MOD_ROWS = 8


def _cparams(n_axes):
    return pltpu.CompilerParams(dimension_semantics=("arbitrary",) * n_axes, vmem_limit_bytes=VMEM_LIMIT)


def _pick(m, cands):
    for c in cands:
        if m % c == 0:
            return c
    raise ValueError(f"no tile for {m} in {cands}")


def _round_up(x, m):
    return (x + m - 1) // m * m


class _Rows:
    def __init__(self, batch, n_ctx, n_lat, sub):
        self.batch, self.n_ctx, self.n_lat, self.sub = batch, n_ctx, n_lat, sub
        self.per_batch = n_ctx + n_lat
        self.total = batch * self.per_batch
        self.spb = self.per_batch // sub
        self.ctx_sub = n_ctx // sub

    def info(self, q):
        b = q // self.spb
        w = q - b * self.spb
        if self.ctx_sub:
            idx = jnp.where(w < self.ctx_sub, self.batch, b)
            start = (w == 0) | (w == self.ctx_sub)
            end = (w == self.ctx_sub - 1) | (w == self.spb - 1)
        else:
            idx = b
            start = w == 0
            end = w == self.spb - 1
        return idx, start, end


def _silu(x):
    return x * jax.nn.sigmoid(x)


def _rms(x):
    return x * lax.rsqrt(jnp.mean(x * x, axis=-1, keepdims=True) + RMS_EPS)


def _ada_kernel(c_ref, w_ref, b_ref, o_ref):
    o_ref[...] = jnp.dot(_silu(c_ref[...]), w_ref[...], preferred_element_type=F32) + b_ref[...]


def _ada(cvec, ada_w, ada_b):
    depth, d, n = ada_w.shape
    tn = _pick(n, (512, 256, 128))
    return pl.pallas_call(
        _ada_kernel,
        grid=(depth, n // tn),
        in_specs=[pl.BlockSpec((MOD_ROWS, d), lambda l, j: (0, 0)),
                  pl.BlockSpec((None, d, tn), lambda l, j: (l, 0, j)),
                  pl.BlockSpec((None, 1, tn), lambda l, j: (l, 0, j))],
        out_specs=pl.BlockSpec((None, MOD_ROWS, tn), lambda l, j: (l, 0, j)),
        out_shape=jax.ShapeDtypeStruct((depth, MOD_ROWS, n), F32),
        compiler_params=_cparams(2),
        name="ada_mod",
    )(cvec, ada_w, ada_b.reshape(depth, 1, n))


def _norm_mod_kernel(x_ref, g_ref, m_ref, o_ref, *, rows, tm, d, sh_col, sc_col):
    i = pl.program_id(0)
    nsub = tm // rows.sub
    for s in range(nsub):
        idx, _, _ = rows.info(i * nsub + s)
        sl = slice(s * rows.sub, (s + 1) * rows.sub)
        sh = m_ref[pl.ds(idx, 1), sh_col * d:(sh_col + 1) * d]
        sc = m_ref[pl.ds(idx, 1), sc_col * d:(sc_col + 1) * d]
        y = _rms(x_ref[sl, :]) * g_ref[...]
        o_ref[sl, :] = (y * (1.0 + sc) + sh).astype(o_ref.dtype)


def _norm_mod(x2, g, mods, rows, sh_col, sc_col, out_dtype):
    m, d = x2.shape
    tm = _pick(m, tuple(t for t in (512, 256, 128, 64, 32, 16, 8) if t % rows.sub == 0))
    return pl.pallas_call(
        functools.partial(_norm_mod_kernel, rows=rows, tm=tm, d=d, sh_col=sh_col, sc_col=sc_col),
        grid=(m // tm,),
        in_specs=[pl.BlockSpec((tm, d), lambda i: (i, 0)),
                  pl.BlockSpec((1, d), lambda i: (0, 0)),
                  pl.BlockSpec(mods.shape, lambda i: (0, 0))],
        out_specs=pl.BlockSpec((tm, d), lambda i: (i, 0)),
        out_shape=jax.ShapeDtypeStruct((m, d), out_dtype),
        compiler_params=_cparams(1),
        name="norm_mod",
    )(x2, g.reshape(1, d), mods)


def _final_norm_kernel(x_ref, g_ref, o_ref):
    o_ref[...] = _rms(x_ref[...]) * g_ref[...]


def _final_norm(x2, g):
    m, d = x2.shape
    tm = _pick(m, (512, 256, 128, 64))
    return pl.pallas_call(
        _final_norm_kernel,
        grid=(m // tm,),
        in_specs=[pl.BlockSpec((tm, d), lambda i: (i, 0)), pl.BlockSpec((1, d), lambda i: (0, 0))],
        out_specs=pl.BlockSpec((tm, d), lambda i: (i, 0)),
        out_shape=jax.ShapeDtypeStruct((m, d), F32),
        compiler_params=_cparams(1),
        name="final_norm",
    )(x2, g.reshape(1, d))


def _edge_masks(rows, i, tm):
    nsub = tm // rows.sub
    rid = lax.broadcasted_iota(jnp.int32, (tm, 1), 0)
    is_start = jnp.zeros((tm, 1), jnp.bool_)
    is_end = jnp.zeros((tm, 1), jnp.bool_)
    for s in range(nsub):
        _, st, en = rows.info(i * nsub + s)
        is_start = is_start | (rid == jnp.where(st, s * rows.sub, -1))
        is_end = is_end | (rid == jnp.where(en, (s + 1) * rows.sub - 1, -1))
    return rid, is_start, is_end


def _shifted(cur, prev_row, next_row, rid, is_start, is_end, tm):
    prev = jnp.where(rid == 0, prev_row, pltpu.roll(cur, 1, 0))
    nxt = jnp.where(rid == tm - 1, next_row, pltpu.roll(cur, tm - 1, 0))
    return jnp.where(is_start, 0.0, prev), jnp.where(is_end, 0.0, nxt)


def _rwkv_prep_kernel(x_ref, xp_ref, xn_ref, g_ref, m_ref, h_ref, xx_ref, *, rows, tm, d):
    i = pl.program_id(0)
    nsub = tm // rows.sub
    total_sub = rows.total // rows.sub

    def mod_rows(q):
        idx, _, _ = rows.info(q)
        return m_ref[pl.ds(idx, 1), 0:d], m_ref[pl.ds(idx, 1), d:2 * d]

    def h_of(x, q):
        sh, sc = mod_rows(q)
        return (_rms(x) * g_ref[...]) * (1.0 + sc) + sh

    for s in range(nsub):
        sl = slice(s * rows.sub, (s + 1) * rows.sub)
        h_ref[sl, :] = h_of(x_ref[sl, :], i * nsub + s)
    h = h_ref[...]
    hp = h_of(xp_ref[SUBLANES - 1:SUBLANES, :], jnp.maximum(i * nsub - 1, 0))
    hn = h_of(xn_ref[0:1, :], jnp.minimum((i + 1) * nsub, total_sub - 1))
    rid, is_start, is_end = _edge_masks(rows, i, tm)
    prev, nxt = _shifted(h, hp, hn, rid, is_start, is_end, tm)
    xx_ref[...] = 0.5 * (prev + nxt) - h


def _rwkv_prep(x2, g, mods, rows):
    m, d = x2.shape
    tm = max(_pick(m, (256, 128, 64)), rows.sub)
    r8 = tm // SUBLANES
    nblk8 = m // SUBLANES
    return pl.pallas_call(
        functools.partial(_rwkv_prep_kernel, rows=rows, tm=tm, d=d),
        grid=(m // tm,),
        in_specs=[pl.BlockSpec((tm, d), lambda i: (i, 0)),
                  pl.BlockSpec((SUBLANES, d), lambda i: (jnp.maximum(i * r8 - 1, 0), 0)),
                  pl.BlockSpec((SUBLANES, d), lambda i: (jnp.minimum((i + 1) * r8, nblk8 - 1), 0)),
                  pl.BlockSpec((1, d), lambda i: (0, 0)),
                  pl.BlockSpec(mods.shape, lambda i: (0, 0))],
        out_specs=[pl.BlockSpec((tm, d), lambda i: (i, 0)), pl.BlockSpec((tm, d), lambda i: (i, 0))],
        out_shape=[jax.ShapeDtypeStruct((m, d), F32), jax.ShapeDtypeStruct((m, d), F32)],
        compiler_params=_cparams(1),
        name="rwkv_prep",
    )(x2, x2, x2, g.reshape(1, d), mods)


def _mm_kernel(*refs, na, ne, no, prologue, epilogue, period):
    a_refs = refs[:na]
    w_ref = refs[na]
    e_refs = refs[na + 1:na + 1 + ne]
    o_refs = refs[na + 1 + ne:na + 1 + ne + no]
    i = pl.program_id(0)
    j = pl.program_id(1)
    if prologue is _first_input:
        lhs = a_refs[0][...]
    else:
        a_scr = refs[-1]

        @pl.when(j % period == 0)
        def _():
            a_scr[...] = prologue(a_refs, i, j // period)

        lhs = a_scr[...]
    acc = jnp.dot(lhs, w_ref[...], preferred_element_type=F32)
    epilogue(acc, e_refs, o_refs, i, j)


def _mm(name, a_arrays, a_specs, prologue, w, e_arrays, e_specs, epilogue, out_shapes, out_specs,
        m, k, n, tm, tn, period=None):
    nj = n // tn
    period = nj if period is None else period
    return pl.pallas_call(
        functools.partial(_mm_kernel, na=len(a_arrays), ne=len(e_arrays), no=len(out_shapes),
                          prologue=prologue, epilogue=epilogue, period=period),
        grid=(m // tm, nj),
        in_specs=list(a_specs) + [pl.BlockSpec((k, tn), lambda i, j: (0, j))] + list(e_specs),
        out_specs=list(out_specs),
        out_shape=list(out_shapes),
        scratch_shapes=[] if prologue is _first_input else [pltpu.VMEM((tm, k), BF16)],
        compiler_params=_cparams(2),
        name=name,
    )(*a_arrays, w, *e_arrays)


def _gated_residual_epilogue(rows, tm):
    nsub = tm // rows.sub

    def epi(acc, e_refs, o_refs, i, j):
        res_ref, m_ref = e_refs
        for s in range(nsub):
            idx, _, _ = rows.info(i * nsub + s)
            sl = slice(s * rows.sub, (s + 1) * rows.sub)
            o_refs[0][sl, :] = res_ref[sl, :] + m_ref[pl.ds(idx, 1), :] * acc[sl, :]
    return epi


def _store_epilogue(acc, e_refs, o_refs, i, j):
    o_refs[0][...] = acc.astype(o_refs[0].dtype)


def _first_input(a_refs, i, grp):
    return a_refs[0][...]


def _proj_residual(name, a_arrays, a_specs, prologue, w, res, mods, rows, gate_col, tm, tn):
    m, d = res.shape
    k = w.shape[0]
    gate_blk = gate_col * (d // tn)
    return _mm(name, a_arrays, a_specs, prologue, w,
               [res, mods],
               [pl.BlockSpec((tm, tn), lambda i, j: (i, j)),
                pl.BlockSpec((MOD_ROWS, tn), lambda i, j: (0, gate_blk + j))],
               _gated_residual_epilogue(rows, tm),
               [jax.ShapeDtypeStruct((m, d), F32)],
               [pl.BlockSpec((tm, tn), lambda i, j: (i, j))],
               m, k, d, tm, tn)[0]


def _softplus(z):
    return jnp.maximum(z, 0.0) + jnp.log(1.0 + jnp.exp(-jnp.abs(z)))


NT_DIMS = (((1,), (1,)), ((), ()))
TN_DIMS = (((0,), (0,)), ((), ()))


def _split_dot(x, ones, terms, ones_left=False):
    acc = None
    rem = x
    for _ in range(terms):
        piece = rem.astype(BF16)
        part = (jnp.dot(ones, piece, preferred_element_type=F32) if ones_left
                else jnp.dot(piece, ones, preferred_element_type=F32))
        acc = part if acc is None else acc + part
        rem = rem - piece.astype(F32)
    return acc


def _chunk_kernel(rf, kf, vf, lwf, icf, rr, kr, vr, lwr, icr, kk_ref, ka_ref, yf_ref, yr_ref, s_ref, *, c, n, hb, w):
    j = pl.program_id(2)

    @pl.when(j == 0)
    def _():
        s_ref[...] = jnp.zeros_like(s_ref)

    hp = w // n
    trow = lax.broadcasted_iota(jnp.int32, (c, w), 0)
    lane = lax.broadcasted_iota(jnp.int32, (c, w), 1)
    scol = lane & (c - 1)
    row = lax.broadcasted_iota(jnp.int32, (c, c), 0)
    col = lax.broadcasted_iota(jnp.int32, (c, c), 1)
    lane_head = lane // n
    seg_b = (lax.broadcasted_iota(jnp.int32, (w, w), 0) // n
             == lax.broadcasted_iota(jnp.int32, (w, w), 1) // n)
    seg = seg_b.astype(F32).astype(BF16)
    doublings = int(math.log2(c))

    def bdiag(x):
        return jnp.concatenate([jnp.where(lane_head == h, x, jnp.zeros_like(x)) for h in range(hp)], axis=0)

    streams = ((rf, kf, vf, lwf, icf, yf_ref, col <= row, scol <= trow, scol < trow, c - 1),
               (rr, kr, vr, lwr, icr, yr_ref, col >= row, scol >= trow, scol > trow, 0))
    chains = []
    for dirn, (r_, k_, v_, lw_, ic_, y_, incl, incl_p, strict_p, last) in enumerate(streams):
        tri = incl.astype(F32).astype(BF16)
        for p in range(hb // hp):
            sl = slice(p * w, (p + 1) * w)
            r, k, v, lw, ic = r_[:, sl], k_[:, sl], v_[:, sl], lw_[:, sl], ic_[:, sl]
            kx = k * kk_ref[:, sl]
            nrm = jnp.sqrt(_split_dot(kx * kx, seg, 3))
            kkn = kx / jnp.maximum(nrm, 1e-12)
            g = _split_dot(lw, tri, 3, ones_left=True)
            eig = jnp.exp(-g)
            kq = ((k * (1.0 + (ic - 1.0) * ka_ref[:, sl])) * eig).astype(BF16)
            bq = ((kkn * ic) * eig).astype(BF16)
            chains.append(dict(
                dirn=dirn, p=p, sl=sl, y_ref=y_, incl_p=incl_p, strict_p=strict_p, eg=jnp.exp(g[last:last + 1, :]),
                vb=v.astype(BF16), qm=jnp.concatenate([kq, bq], axis=0),
                pm=jnp.concatenate([r * jnp.exp(g), -kkn * jnp.exp(g - lw)], axis=0).astype(BF16),
                qbd=jnp.concatenate([bdiag(kq), bdiag(bq)], axis=0)))
    for ch in chains:
        gp = lax.dot_general(ch["pm"], ch["qbd"], NT_DIMS, preferred_element_type=F32)
        ch["a_r"] = jnp.concatenate([jnp.where(ch["incl_p"], gp[:c, :w], 0.0),
                                     jnp.where(ch["strict_p"], gp[c:, :w], 0.0)], axis=0).astype(BF16)
        ch["a_rb"] = jnp.where(ch["incl_p"], gp[:c, w:], 0.0).astype(BF16)
        ch["m"] = jnp.where(ch["strict_p"], gp[c:, w:], 0.0)
    for ch in chains:
        ch["s0"] = s_ref[ch["dirn"], ch["p"]]
        ch["ps"] = lax.dot_general(ch["pm"], ch["s0"].astype(BF16), NT_DIMS, preferred_element_type=F32)
    for ch in chains:
        t = ch["ps"] + jnp.dot(ch["a_r"], bdiag(ch["vb"]), preferred_element_type=F32)
        ch["y0"], ch["x"] = t[:c], t[c:]
    for it in range(doublings):
        for ch in chains:
            m, x = ch["m"], ch["x"]
            mb, xb = m.astype(BF16), x.astype(BF16)
            xl = (x - xb.astype(F32)).astype(BF16)
            ch["x"] = x + jnp.dot(jnp.concatenate([mb, mb], axis=1), jnp.concatenate([bdiag(xb), bdiag(xl)], axis=0),
                                  preferred_element_type=F32)
            if it + 1 < doublings:
                ch["m"] = jnp.dot(mb, bdiag(mb), preferred_element_type=F32)
    for ch in chains:
        ub = ch["x"].astype(BF16)
        ch["y_ref"][:, ch["sl"]] = ch["y0"] + jnp.dot(ch["a_rb"], bdiag(ub), preferred_element_type=F32)
        vu = jnp.concatenate([ch["vb"], ub], axis=0)
        upd = lax.dot_general(vu, ch["qm"], TN_DIMS, preferred_element_type=F32)
        s_ref[ch["dirn"], ch["p"]] = (ch["s0"] + jnp.where(seg_b, upd, 0.0)) * ch["eg"]


def _rwkv_chunked(rkv, logw, icl, k_k, k_a, b, lt, n_ctx, h, n):
    d = h * n
    c = CHUNK
    hb = min(CHUNK_HEADS, h)
    slab = min(CHUNK_SLAB, hb * n)
    assert c == n and slab % n == 0 and (hb * n) % slab == 0, "slab layout: chunk length equals the head size"
    nch, nc, ng, w = lt // c, n_ctx // c, h // hb, hb * n

    def rev(j):
        return jnp.where(j < nc, nc - 1 - j, nch - 1 - (j - nc))

    def spec(tmap, colgrp):
        return pl.BlockSpec((c, w), lambda bb, g, j: (bb * nch + tmap(j), colgrp * ng + g))

    ident = lambda j: j
    const = pl.BlockSpec((1, w), lambda bb, g, j: (0, g))
    return pl.pallas_call(
        functools.partial(_chunk_kernel, c=c, n=n, hb=hb, w=slab),
        grid=(b, ng, nch),
        in_specs=[spec(ident, 0), spec(ident, 1), spec(ident, 2), spec(ident, 0), spec(ident, 0),
                  spec(rev, 0), spec(rev, 1), spec(rev, 2), spec(rev, 1), spec(rev, 1), const, const],
        out_specs=[spec(ident, 0), spec(rev, 0)],
        out_shape=[jax.ShapeDtypeStruct((b * lt, d), F32)] * 2,
        scratch_shapes=[pltpu.VMEM((2, hb * n // slab, slab, slab), F32)],
        compiler_params=_cparams(3),
        name="rwkv_chunk",
    )(rkv, rkv, rkv, logw, icl, rkv, rkv, rkv, logw, icl, k_k.reshape(1, d), k_a.reshape(1, d))


def _readout_tok_kernel(yf_ref, yr_ref, r_ref, k_ref, v_ref, a0_ref, a1_ref, g_ref, ka_ref, rk_ref, gw_ref, gb_ref,
                        o_ref, *, n, d):
    seg = (lax.broadcasted_iota(jnp.int32, (LANES, LANES), 0) // n
           == lax.broadcasted_iota(jnp.int32, (LANES, LANES), 1) // n).astype(F32).astype(BF16)

    def seg_sum(t):
        return _split_dot(t, seg, 2)

    for p in range(d // LANES):
        sl = slice(p * LANES, (p + 1) * LANES)
        y = yf_ref[:, sl] + yr_ref[:, sl]
        yc = y - seg_sum(y) * (1.0 / n)
        var = seg_sum(yc * yc) * (1.0 / n)
        yn = (yc * lax.rsqrt(var + GN_EPS)) * gw_ref[:, sl] + gb_ref[:, sl]
        k = k_ref[:, sl]
        ka = ka_ref[:, sl]
        kd = k * (1.0 + (a0_ref[:, sl] - 1.0) * ka) + k * (1.0 + (a1_ref[:, sl] - 1.0) * ka)
        bonus = seg_sum(r_ref[:, sl] * rk_ref[:, sl] * kd)
        o_ref[:, sl] = ((yn + bonus * v_ref[:, sl]) * g_ref[:, sl]).astype(o_ref.dtype)


def _rwkv_readout_tok(yf, yr, rkv, icl, gate, k_a, r_k, gn_w, gn_b, n):
    m, d = yf.shape
    tm = _pick(m, (128, 64))
    blk = lambda cg: pl.BlockSpec((tm, d), lambda i: (i, cg))
    const = pl.BlockSpec((1, d), lambda i: (0, 0))
    return pl.pallas_call(
        functools.partial(_readout_tok_kernel, n=n, d=d),
        grid=(m // tm,),
        in_specs=[blk(0), blk(0), blk(0), blk(1), blk(2), blk(0), blk(1), blk(0)] + [const] * 4,
        out_specs=blk(0),
        out_shape=jax.ShapeDtypeStruct((m, d), BF16),
        compiler_params=_cparams(1),
        name="rwkv_readout",
    )(yf, yr, rkv, rkv, rkv, icl, icl, gate, k_a.reshape(1, d), r_k.reshape(1, d), gn_w.reshape(1, d), gn_b.reshape(1, d))


def _ffn_up_kernel(h_ref, hp_ref, hn_ref, wg_ref, wv_ref, cw_ref, cb_ref, o_ref, lhs_ref, *, rows, tm):
    i = pl.program_id(0)

    @pl.when(pl.program_id(1) == 0)
    def _():
        lhs_ref[0:tm, :] = h_ref[...]
        lhs_ref[tm:tm + BF16_ROWS, :] = hp_ref[...]
        lhs_ref[tm + BF16_ROWS:tm + 2 * BF16_ROWS, :] = hn_ref[...]

    gate_all = jnp.dot(lhs_ref[...], wg_ref[...], preferred_element_type=F32)
    gate = gate_all[:tm]
    gp = gate_all[tm + BF16_ROWS - 1:tm + BF16_ROWS, :]
    gn = gate_all[tm + BF16_ROWS:tm + BF16_ROWS + 1, :]
    val = jnp.dot(h_ref[...], wv_ref[...], preferred_element_type=F32)
    rid, is_start, is_end = _edge_masks(rows, i, tm)
    prev, nxt = _shifted(gate, gp, gn, rid, is_start, is_end, tm)
    conv = prev * cw_ref[0:1, :] + gate * cw_ref[1:2, :] + nxt * cw_ref[2:3, :] + cb_ref[...]
    o_ref[...] = (_silu(conv) * val).astype(o_ref.dtype)


def _ffn_up(hf, wup, conv_w, conv_b, rows, tm, tn):
    m, d = hf.shape
    f = wup.shape[1] // 2
    r8 = tm // BF16_ROWS
    nblk8 = m // BF16_ROWS
    njf = f // tn
    cw = jnp.zeros((SUBLANES, f), F32).at[:conv_w.shape[0]].set(conv_w)
    return pl.pallas_call(
        functools.partial(_ffn_up_kernel, rows=rows, tm=tm),
        grid=(m // tm, njf),
        in_specs=[pl.BlockSpec((tm, d), lambda i, j: (i, 0)),
                  pl.BlockSpec((BF16_ROWS, d), lambda i, j: (jnp.maximum(i * r8 - 1, 0), 0)),
                  pl.BlockSpec((BF16_ROWS, d), lambda i, j: (jnp.minimum((i + 1) * r8, nblk8 - 1), 0)),
                  pl.BlockSpec((d, tn), lambda i, j: (0, j)),
                  pl.BlockSpec((d, tn), lambda i, j: (0, j + njf)),
                  pl.BlockSpec((SUBLANES, tn), lambda i, j: (0, j)),
                  pl.BlockSpec((1, tn), lambda i, j: (0, j))],
        out_specs=pl.BlockSpec((tm, tn), lambda i, j: (i, j)),
        out_shape=jax.ShapeDtypeStruct((m, f), BF16),
        scratch_shapes=[pltpu.VMEM((tm + 2 * BF16_ROWS, d), BF16)],
        compiler_params=_cparams(2),
        name="ffn_up",
    )(hf, hf, hf, wup, wup, cw, conv_b.reshape(1, f))


def _conv_ffn(x2, g, mods, rows, wup, conv_w, conv_b, wdown, tm):
    m, d = x2.shape
    f = wdown.shape[0]
    hf = _norm_mod(x2, g, mods, rows, 3, 4, BF16)
    act = _ffn_up(hf, wup.astype(BF16), conv_w, conv_b, rows, tm, _pick(f, (512, 256, 128)))
    tmd = _pick(m, tuple(t for t in (1024, 512, 256, 128, 64) if t <= tm and t % rows.sub == 0))
    return _proj_residual("ffn_down", [act], [pl.BlockSpec((tmd, f), lambda i, j: (i, 0))],
                          _first_input, wdown.astype(BF16), x2, mods, rows, 5,
                          tmd, _pick(d, (512, 256, 128)))


def _rope(x, cos, sin_signed):
    lane = lax.broadcasted_iota(jnp.int32, x.shape, 1)
    first = (lane & 31) < 16
    swapped = jnp.where(first, pltpu.roll(x, LANES - 16, 1), pltpu.roll(x, 16, 1))
    return x * cos + swapped * sin_signed


def _kv_up_kernel(c_ref, wk_ref, wv_ref, kr_ref, ko_ref, vo_ref, *, hpt):
    c = c_ref[...]
    kn = jnp.dot(c, wk_ref[...], preferred_element_type=F32).astype(BF16)
    vo_ref[...] = jnp.dot(c, wv_ref[...], preferred_element_type=F32).astype(BF16)
    kr = kr_ref[...]
    for h in range(hpt):
        ko_ref[:, h * 2 * LANES:h * 2 * LANES + QK_NOPE] = kn[:, h * QK_NOPE:(h + 1) * QK_NOPE]
        ko_ref[:, h * 2 * LANES + QK_NOPE:(h + 1) * 2 * LANES] = kr


def _attn_kernel(q_ref, k_ref, v_ref, o_ref, *, hpa, hq):
    for h in range(hpa):
        q = q_ref[:, h * hq:(h + 1) * hq]
        k = k_ref[:, h * hq:(h + 1) * hq]
        s = lax.dot_general(q, k, NT_DIMS, preferred_element_type=F32)
        p = jnp.exp2(s - jnp.max(s, axis=-1, keepdims=True))
        denom = jnp.sum(p, axis=-1, keepdims=True)
        o = jnp.dot(p.astype(BF16), v_ref[:, h * V_HEAD:(h + 1) * V_HEAD], preferred_element_type=F32)
        o_ref[:, h * V_HEAD:(h + 1) * V_HEAD] = (o / denom).astype(o_ref.dtype)


def _rwkv_layer(xa, mods, norm_g, rows, p, tm):
    (mu, wr, wk, wv, wo, w0, w1, w2, a0, a1, a2, g1, g2, k_k, k_a, r_k, gn_w, gn_b) = p
    m, d = xa.shape
    h_heads, n = r_k.shape
    b, lt = rows.batch, rows.per_batch
    glora, dlora, alora = g1.shape[1], w1.shape[2], a1.shape[2]
    gw = _round_up(max(glora, 2 * dlora, 2 * alora), LANES)

    h, xx = _rwkv_prep(xa, norm_g, mods, rows)

    a_specs = [pl.BlockSpec((tm, d), lambda i, j: (i, 0)), pl.BlockSpec((tm, d), lambda i, j: (i, 0)),
               pl.BlockSpec((MOD_ROWS, d), lambda i, j: (0, 0))]

    def mix(a, i, grp):
        return (a[0][...] + a[1][...] * a[2][pl.ds(grp, 1), :]).astype(BF16)

    def pad_rows(t):
        return jnp.zeros((MOD_ROWS, d), F32).at[:t.shape[0]].set(t)

    tn = _pick(d, (512, 256, 128))
    w_rkv = jnp.concatenate([wr, wk, wv], axis=1).astype(BF16)
    rkv = _mm("rwkv_rkv", [h, xx, pad_rows(mu[jnp.array([0, 2, 3])])], a_specs, mix, w_rkv, [], [],
              _store_epilogue,
              [jax.ShapeDtypeStruct((m, 3 * d), F32)], [pl.BlockSpec((tm, tn), lambda i, j: (i, j))],
              m, d, 3 * d, tm, tn, period=d // tn)[0]

    def pad_cols(t):
        return jnp.pad(t, ((0, 0), (0, gw - t.shape[1])))

    w_lora = jnp.concatenate([pad_cols(g1), pad_cols(jnp.concatenate([w1[0], w1[1]], axis=1)),
                              pad_cols(jnp.concatenate([a1[0], a1[1]], axis=1))], axis=1).astype(BF16)

    def lora_act(acc, e, o, i, j):
        o[0][...] = jnp.where(j == 0, jax.nn.sigmoid(acc), jnp.where(j == 1, jnp.tanh(acc), acc)).astype(BF16)

    hid = _mm("rwkv_lora1", [h, xx, pad_rows(mu[jnp.array([5, 1, 4])])], a_specs, mix, w_lora, [], [], lora_act,
              [jax.ShapeDtypeStruct((m, 3 * gw), BF16)], [pl.BlockSpec((tm, gw), lambda i, j: (i, j))],
              m, d, 3 * gw, tm, gw, period=1)[0]

    def hid_spec(grp):
        return [pl.BlockSpec((tm, gw), lambda i, j: (i, grp))]

    ident = _first_input
    g2p = jnp.zeros((gw, d), F32).at[:glora].set(g2).astype(BF16)
    gate = _mm("rwkv_gate", [hid], hid_spec(0), ident, g2p, [], [],
               _store_epilogue,
               [jax.ShapeDtypeStruct((m, d), F32)], [pl.BlockSpec((tm, tn), lambda i, j: (i, j))],
               m, gw, d, tm, tn)[0]

    def two_dir(t2, lora):
        z = jnp.zeros((gw, 2 * d), F32)
        z = z.at[:lora, :d].set(t2[0]).at[lora:2 * lora, d:].set(t2[1])
        return z.astype(BF16)

    bias_spec = [pl.BlockSpec((1, tn), lambda i, j: (0, j))]

    def decay_epi(acc, e, o, i, j):
        w_log = -_softplus(-(e[0][...] + acc)) - 0.5
        o[0][...] = -jnp.exp(w_log)

    logw = _mm("rwkv_decay", [hid], hid_spec(1), ident, two_dir(w2, dlora), [w0.reshape(1, 2 * d)], bias_spec,
               decay_epi, [jax.ShapeDtypeStruct((m, 2 * d), F32)], [pl.BlockSpec((tm, tn), lambda i, j: (i, j))],
               m, gw, 2 * d, tm, tn)[0]

    def iclr_epi(acc, e, o, i, j):
        o[0][...] = jax.nn.sigmoid(e[0][...] + acc)

    icl = _mm("rwkv_iclr", [hid], hid_spec(2), ident, two_dir(a2, alora), [a0.reshape(1, 2 * d)], bias_spec,
              iclr_epi, [jax.ShapeDtypeStruct((m, 2 * d), F32)], [pl.BlockSpec((tm, tn), lambda i, j: (i, j))],
              m, gw, 2 * d, tm, tn)[0]

    yf, yr = _rwkv_chunked(rkv, logw, icl, k_k, k_a, b, lt, rows.n_ctx, h_heads, n)
    og = _rwkv_readout_tok(yf, yr, rkv, icl, gate, k_a, r_k.reshape(d), gn_w, gn_b, n)
    return _proj_residual("rwkv_out", [og], [pl.BlockSpec((tm, d), lambda i, j: (i, 0))], _first_input,
                          wo.astype(BF16), xa, mods, rows, 2, tm, tn)


def _mla_layer(xa, xl, mods, norm_g, rows_u, rows_l, p, tm_u, tm_l):
    wdown, qnorm, kvnorm, wuq, wukv, wo = p
    m, d = xa.shape
    b, lt, lc, ll = rows_u.batch, rows_u.per_batch, rows_u.n_ctx, rows_u.n_lat
    ql, kvl = qnorm.shape[0], kvnorm.shape[0]
    heads = wuq.shape[1] // (QK_NOPE + QK_ROPE)
    hq = 2 * LANES

    npairs = QK_ROPE // 4
    pos = jnp.arange(ll)
    inv_freq = jnp.float32(ROPE_THETA) ** (-jnp.arange(npairs, dtype=F32) / npairs)
    ang_r = (pos // GRID_W).astype(F32)[:, None] * inv_freq
    ang_c = (pos % GRID_W).astype(F32)[:, None] * inv_freq
    cos64 = jnp.concatenate([jnp.cos(ang_r)] * 2 + [jnp.cos(ang_c)] * 2, axis=1)
    sin64 = jnp.concatenate([-jnp.sin(ang_r), jnp.sin(ang_r), -jnp.sin(ang_c), jnp.sin(ang_c)], axis=1)
    padl = ((lc, 0), (0, LANES - QK_ROPE))
    cos_t = jnp.tile(jnp.pad(cos64, padl, constant_values=1.0), (b, 1))
    sin_t = jnp.tile(jnp.pad(sin64, padl), (b, 1))

    h1 = _norm_mod(xa, norm_g, mods, rows_u, 0, 1, BF16)

    nd = _round_up(ql + kvl + QK_ROPE, LANES)
    wd = jnp.pad(wdown, ((0, 0), (0, nd - wdown.shape[1]))).astype(BF16)

    def down_epi(acc, e, o, i, j):
        qn_ref, kvn_ref, cos_ref, sin_ref = e
        o[0][...] = (_rms(acc[:, :ql]) * qn_ref[...]).astype(BF16)
        o[1][...] = (_rms(acc[:, ql:ql + kvl]) * kvn_ref[...]).astype(BF16)
        o[2][...] = _rope(acc[:, ql + kvl:], cos_ref[...], sin_ref[...]).astype(BF16)

    tab_spec = pl.BlockSpec((tm_u, LANES), lambda i, j: (i, 0))
    cq, ckv, krope = _mm(
        "mla_down", [h1], [pl.BlockSpec((tm_u, d), lambda i, j: (i, 0))], _first_input, wd,
        [qnorm.reshape(1, ql), kvnorm.reshape(1, kvl), cos_t, sin_t],
        [pl.BlockSpec((1, ql), lambda i, j: (0, 0)), pl.BlockSpec((1, kvl), lambda i, j: (0, 0)), tab_spec, tab_spec],
        down_epi,
        [jax.ShapeDtypeStruct((m, ql), BF16), jax.ShapeDtypeStruct((m, kvl), BF16),
         jax.ShapeDtypeStruct((m, LANES), BF16)],
        [pl.BlockSpec((tm_u, ql), lambda i, j: (i, 0)), pl.BlockSpec((tm_u, kvl), lambda i, j: (i, 0)),
         pl.BlockSpec((tm_u, LANES), lambda i, j: (i, 0))],
        m, d, nd, tm_u, nd)

    wq = wuq.reshape(ql, heads, QK_NOPE + QK_ROPE)
    wq = jnp.pad(wq, ((0, 0), (0, 0), (0, hq - QK_NOPE - QK_ROPE))).reshape(ql, heads * hq).astype(BF16)
    hpt = _pick(heads, (4, 2, 1))

    qscale = (QK_NOPE + QK_ROPE) ** -0.5 * math.log2(math.e)

    def q_epi(acc, e, o, i, j):
        cos_ref, sin_ref = e
        acc = acc * qscale
        for hh in range(hpt):
            lo = hh * hq
            o[0][:, lo:lo + QK_NOPE] = acc[:, lo:lo + QK_NOPE].astype(BF16)
            o[0][:, lo + QK_NOPE:lo + hq] = _rope(acc[:, lo + QK_NOPE:lo + hq], cos_ref[...], sin_ref[...]).astype(BF16)

    q = _mm("mla_q_up", [cq], [pl.BlockSpec((tm_u, ql), lambda i, j: (i, 0))], _first_input, wq,
            [cos_t, sin_t], [tab_spec, tab_spec], q_epi,
            [jax.ShapeDtypeStruct((m, heads * hq), BF16)], [pl.BlockSpec((tm_u, hpt * hq), lambda i, j: (i, j))],
            m, ql, heads * hq, tm_u, hpt * hq)[0]

    wkv = wukv.reshape(kvl, heads, QK_NOPE + V_HEAD)
    wk = wkv[:, :, :QK_NOPE].reshape(kvl, heads * QK_NOPE).astype(BF16)
    wv = wkv[:, :, QK_NOPE:].reshape(kvl, heads * V_HEAD).astype(BF16)
    kcat, vals = pl.pallas_call(
        functools.partial(_kv_up_kernel, hpt=hpt),
        grid=(m // tm_u, heads // hpt),
        in_specs=[pl.BlockSpec((tm_u, kvl), lambda i, j: (i, 0)),
                  pl.BlockSpec((kvl, hpt * QK_NOPE), lambda i, j: (0, j)),
                  pl.BlockSpec((kvl, hpt * V_HEAD), lambda i, j: (0, j)),
                  pl.BlockSpec((tm_u, LANES), lambda i, j: (i, 0))],
        out_specs=[pl.BlockSpec((tm_u, hpt * hq), lambda i, j: (i, j)),
                   pl.BlockSpec((tm_u, hpt * V_HEAD), lambda i, j: (i, j))],
        out_shape=[jax.ShapeDtypeStruct((m, heads * hq), BF16), jax.ShapeDtypeStruct((m, heads * V_HEAD), BF16)],
        compiler_params=_cparams(2),
        name="mla_kv_up",
    )(ckv, wk, wv, krope)

    tq = _pick(math.gcd(lc, ll), (256, 128, 64))
    hpa = _pick(heads, (2, 1))
    o = pl.pallas_call(
        functools.partial(_attn_kernel, hpa=hpa, hq=hq),
        grid=(b, heads // hpa, ll // tq),
        in_specs=[pl.BlockSpec((None, tq, hpa * hq), lambda bb, hh, i: (bb, lc // tq + i, hh)),
                  pl.BlockSpec((None, lt, hpa * hq), lambda bb, hh, i: (bb, 0, hh)),
                  pl.BlockSpec((None, lt, hpa * V_HEAD), lambda bb, hh, i: (bb, 0, hh))],
        out_specs=pl.BlockSpec((None, tq, hpa * V_HEAD), lambda bb, hh, i: (bb, i, hh)),
        out_shape=jax.ShapeDtypeStruct((b, ll, heads * V_HEAD), BF16),
        compiler_params=_cparams(3),
        name="mla_attn",
    )(q.reshape(b, lt, heads * hq), kcat.reshape(b, lt, heads * hq), vals.reshape(b, lt, heads * V_HEAD))

    ko = heads * V_HEAD
    return _proj_residual("mla_out", [o.reshape(b * ll, ko)], [pl.BlockSpec((tm_l, ko), lambda i, j: (i, 0))],
                          _first_input, wo.astype(BF16), xl, mods, rows_l, 2,
                          tm_l, _pick(d, (512, 256, 128)))


def kernel(x, c, ctx, c_ctx, ada_w, ada_b, norm_g, final_g, rk_mu, rk_wr, rk_wk, rk_wv, rk_wo, rk_w0, rk_w1, rk_w2, rk_a0, rk_a1, rk_a2, rk_g1, rk_g2, rk_kk, rk_ka, rk_rk, rk_gn_w, rk_gn_b, ml_wdown, ml_qnorm, ml_kvnorm, ml_wuq, ml_wukv, ml_wo, ff_wup, ff_conv, ff_convb, ff_wdown):
    b, ll, d = x.shape
    lc = ctx.shape[1]
    depth = ada_w.shape[0]
    heads_r = rk_rk.shape[1]
    assert depth == 2 and rk_mu.shape[0] == 1 and ml_wdown.shape[0] == 1, "one RWKV layer then one MLA layer"
    assert b + 1 <= MOD_ROWS and b * heads_r <= LANES
    sub = _pick(math.gcd(lc, ll), (256, 128, 64, 32, 16, 8))
    rows_u = _Rows(b, lc, ll, sub)
    rows_l = _Rows(b, 0, ll, sub)
    tm_u = _pick(rows_u.total, tuple(t for t in (1024, 512, 256, 128, 64) if t % sub == 0))
    tm_l = _pick(rows_l.total, tuple(t for t in (1024, 512, 256, 128, 64) if t % sub == 0))

    cvec = jnp.zeros((MOD_ROWS, d), F32).at[:b].set(c).at[b].set(c_ctx)
    mods = _ada(cvec, ada_w, ada_b)

    xa = jnp.concatenate([ctx, x], axis=1).reshape(rows_u.total, d)

    p0 = (rk_mu[0], rk_wr[0], rk_wk[0], rk_wv[0], rk_wo[0], rk_w0[0], rk_w1[0], rk_w2[0], rk_a0[0], rk_a1[0],
          rk_a2[0], rk_g1[0], rk_g2[0], rk_kk[0], rk_ka[0], rk_rk[0], rk_gn_w[0], rk_gn_b[0])
    xa = _rwkv_layer(xa, mods[0], norm_g[0, 0], rows_u, p0, tm_u)
    xa = _conv_ffn(xa, norm_g[0, 1], mods[0], rows_u, ff_wup[0], ff_conv[0], ff_convb[0], ff_wdown[0], tm_u)

    xl = xa.reshape(b, rows_u.per_batch, d)[:, lc:].reshape(rows_l.total, d)
    p1 = (ml_wdown[0], ml_qnorm[0], ml_kvnorm[0], ml_wuq[0], ml_wukv[0], ml_wo[0])
    xl = _mla_layer(xa, xl, mods[1], norm_g[1, 0], rows_u, rows_l, p1, tm_u, tm_l)
    xl = _conv_ffn(xl, norm_g[1, 1], mods[1], rows_l, ff_wup[1], ff_conv[1], ff_convb[1], ff_wdown[1], tm_l)
    return _final_norm(xl, final_g).reshape(b, ll, d)
```

```python
import functools
import math

import jax
import jax.numpy as jnp
from jax import lax
from jax.experimental import pallas as pl
from jax.experimental.pallas import tpu as pltpu

F32 = jnp.float32
BF16 = jnp.bfloat16

RMS_EPS = 1e-6
GN_EPS = 64e-5
QK_NOPE = 128
QK_ROPE = 64
V_HEAD = 128
GRID_W = 64
ROPE_THETA = 10000.0
LANES = 128
SUBLANES = 8
BF16_ROWS = 16
CHUNK = 64
CHUNK_HEADS = 32
CHUNK_SLAB = 128
VMEM_LIMIT = 56 * 1024 * 1024
MOD_ROWS = 8


def _cparams(n_axes):
    return pltpu.CompilerParams(dimension_semantics=("arbitrary",) * n_axes, vmem_limit_bytes=VMEM_LIMIT)


def _pick(m, cands):
    for c in cands:
        if m % c == 0:
            return c
    raise ValueError(f"no tile for {m} in {cands}")


def _round_up(x, m):
    return (x + m - 1) // m * m


class _Rows:
    def __init__(self, batch, n_ctx, n_lat, sub):
        self.batch, self.n_ctx, self.n_lat, self.sub = batch, n_ctx, n_lat, sub
        self.per_batch = n_ctx + n_lat
        self.total = batch * self.per_batch
        self.spb = self.per_batch // sub
        self.ctx_sub = n_ctx // sub

    def info(self, q):
        b = q // self.spb
        w = q - b * self.spb
        if self.ctx_sub:
            idx = jnp.where(w < self.ctx_sub, self.batch, b)
            start = (w == 0) | (w == self.ctx_sub)
            end = (w == self.ctx_sub - 1) | (w == self.spb - 1)
        else:
            idx = b
            start = w == 0
            end = w == self.spb - 1
        return idx, start, end


def _silu(x):
    return x * jax.nn.sigmoid(x)


def _rms(x):
    return x * lax.rsqrt(jnp.mean(x * x, axis=-1, keepdims=True) + RMS_EPS)


def _ada_kernel(c_ref, w_ref, b_ref, o_ref):
    o_ref[...] = jnp.dot(_silu(c_ref[...]), w_ref[...], preferred_element_type=F32) + b_ref[...]


def _ada(cvec, ada_w, ada_b):
    depth, d, n = ada_w.shape
    tn = _pick(n, (512, 256, 128))
    return pl.pallas_call(
        _ada_kernel,
        grid=(depth, n // tn),
        in_specs=[pl.BlockSpec((MOD_ROWS, d), lambda l, j: (0, 0)),
                  pl.BlockSpec((None, d, tn), lambda l, j: (l, 0, j)),
                  pl.BlockSpec((None, 1, tn), lambda l, j: (l, 0, j))],
        out_specs=pl.BlockSpec((None, MOD_ROWS, tn), lambda l, j: (l, 0, j)),
        out_shape=jax.ShapeDtypeStruct((depth, MOD_ROWS, n), F32),
        compiler_params=_cparams(2),
        name="ada_mod",
    )(cvec, ada_w, ada_b.reshape(depth, 1, n))


def _norm_mod_kernel(x_ref, g_ref, m_ref, o_ref, *, rows, tm, d, sh_col, sc_col):
    i = pl.program_id(0)
    nsub = tm // rows.sub
    for s in range(nsub):
        idx, _, _ = rows.info(i * nsub + s)
        sl = slice(s * rows.sub, (s + 1) * rows.sub)
        sh = m_ref[pl.ds(idx, 1), sh_col * d:(sh_col + 1) * d]
        sc = m_ref[pl.ds(idx, 1), sc_col * d:(sc_col + 1) * d]
        y = _rms(x_ref[sl, :]) * g_ref[...]
        o_ref[sl, :] = (y * (1.0 + sc) + sh).astype(o_ref.dtype)


def _norm_mod(x2, g, mods, rows, sh_col, sc_col, out_dtype):
    m, d = x2.shape
    tm = _pick(m, tuple(t for t in (512, 256, 128, 64, 32, 16, 8) if t % rows.sub == 0))
    return pl.pallas_call(
        functools.partial(_norm_mod_kernel, rows=rows, tm=tm, d=d, sh_col=sh_col, sc_col=sc_col),
        grid=(m // tm,),
        in_specs=[pl.BlockSpec((tm, d), lambda i: (i, 0)),
                  pl.BlockSpec((1, d), lambda i: (0, 0)),
                  pl.BlockSpec(mods.shape, lambda i: (0, 0))],
        out_specs=pl.BlockSpec((tm, d), lambda i: (i, 0)),
        out_shape=jax.ShapeDtypeStruct((m, d), out_dtype),
        compiler_params=_cparams(1),
        name="norm_mod",
    )(x2, g.reshape(1, d), mods)


def _final_norm_kernel(x_ref, g_ref, o_ref):
    o_ref[...] = _rms(x_ref[...]) * g_ref[...]


def _final_norm(x2, g):
    m, d = x2.shape
    tm = _pick(m, (512, 256, 128, 64))
    return pl.pallas_call(
        _final_norm_kernel,
        grid=(m // tm,),
        in_specs=[pl.BlockSpec((tm, d), lambda i: (i, 0)), pl.BlockSpec((1, d), lambda i: (0, 0))],
        out_specs=pl.BlockSpec((tm, d), lambda i: (i, 0)),
        out_shape=jax.ShapeDtypeStruct((m, d), F32),
        compiler_params=_cparams(1),
        name="final_norm",
    )(x2, g.reshape(1, d))


def _edge_masks(rows, i, tm):
    nsub = tm // rows.sub
    rid = lax.broadcasted_iota(jnp.int32, (tm, 1), 0)
    is_start = jnp.zeros((tm, 1), jnp.bool_)
    is_end = jnp.zeros((tm, 1), jnp.bool_)
    for s in range(nsub):
        _, st, en = rows.info(i * nsub + s)
        is_start = is_start | (rid == jnp.where(st, s * rows.sub, -1))
        is_end = is_end | (rid == jnp.where(en, (s + 1) * rows.sub - 1, -1))
    return rid, is_start, is_end


def _shifted(cur, prev_row, next_row, rid, is_start, is_end, tm):
    prev = jnp.where(rid == 0, prev_row, pltpu.roll(cur, 1, 0))
    nxt = jnp.where(rid == tm - 1, next_row, pltpu.roll(cur, tm - 1, 0))
    return jnp.where(is_start, 0.0, prev), jnp.where(is_end, 0.0, nxt)


def _rwkv_prep_kernel(x_ref, xp_ref, xn_ref, g_ref, m_ref, h_ref, xx_ref, *, rows, tm, d):
    i = pl.program_id(0)
    nsub = tm // rows.sub
    total_sub = rows.total // rows.sub

    def mod_rows(q):
        idx, _, _ = rows.info(q)
        return m_ref[pl.ds(idx, 1), 0:d], m_ref[pl.ds(idx, 1), d:2 * d]

    def h_of(x, q):
        sh, sc = mod_rows(q)
        return (_rms(x) * g_ref[...]) * (1.0 + sc) + sh

    for s in range(nsub):
        sl = slice(s * rows.sub, (s + 1) * rows.sub)
        h_ref[sl, :] = h_of(x_ref[sl, :], i * nsub + s)
    h = h_ref[...]
    hp = h_of(xp_ref[SUBLANES - 1:SUBLANES, :], jnp.maximum(i * nsub - 1, 0))
    hn = h_of(xn_ref[0:1, :], jnp.minimum((i + 1) * nsub, total_sub - 1))
    rid, is_start, is_end = _edge_masks(rows, i, tm)
    prev, nxt = _shifted(h, hp, hn, rid, is_start, is_end, tm)
    xx_ref[...] = 0.5 * (prev + nxt) - h


def _rwkv_prep(x2, g, mods, rows):
    m, d = x2.shape
    tm = max(_pick(m, (256, 128, 64)), rows.sub)
    r8 = tm // SUBLANES
    nblk8 = m // SUBLANES
    return pl.pallas_call(
        functools.partial(_rwkv_prep_kernel, rows=rows, tm=tm, d=d),
        grid=(m // tm,),
        in_specs=[pl.BlockSpec((tm, d), lambda i: (i, 0)),
                  pl.BlockSpec((SUBLANES, d), lambda i: (jnp.maximum(i * r8 - 1, 0), 0)),
                  pl.BlockSpec((SUBLANES, d), lambda i: (jnp.minimum((i + 1) * r8, nblk8 - 1), 0)),
                  pl.BlockSpec((1, d), lambda i: (0, 0)),
                  pl.BlockSpec(mods.shape, lambda i: (0, 0))],
        out_specs=[pl.BlockSpec((tm, d), lambda i: (i, 0)), pl.BlockSpec((tm, d), lambda i: (i, 0))],
        out_shape=[jax.ShapeDtypeStruct((m, d), F32), jax.ShapeDtypeStruct((m, d), F32)],
        compiler_params=_cparams(1),
        name="rwkv_prep",
    )(x2, x2, x2, g.reshape(1, d), mods)


def _mm_kernel(*refs, na, ne, no, prologue, epilogue, period):
    a_refs = refs[:na]
    w_ref = refs[na]
    e_refs = refs[na + 1:na + 1 + ne]
    o_refs = refs[na + 1 + ne:na + 1 + ne + no]
    i = pl.program_id(0)
    j = pl.program_id(1)
    if prologue is _first_input:
        lhs = a_refs[0][...]
    else:
        a_scr = refs[-1]

        @pl.when(j % period == 0)
        def _():
            a_scr[...] = prologue(a_refs, i, j // period)

        lhs = a_scr[...]
    acc = jnp.dot(lhs, w_ref[...], preferred_element_type=F32)
    epilogue(acc, e_refs, o_refs, i, j)


def _mm(name, a_arrays, a_specs, prologue, w, e_arrays, e_specs, epilogue, out_shapes, out_specs,
        m, k, n, tm, tn, period=None):
    nj = n // tn
    period = nj if period is None else period
    return pl.pallas_call(
        functools.partial(_mm_kernel, na=len(a_arrays), ne=len(e_arrays), no=len(out_shapes),
                          prologue=prologue, epilogue=epilogue, period=period),
        grid=(m // tm, nj),
        in_specs=list(a_specs) + [pl.BlockSpec((k, tn), lambda i, j: (0, j))] + list(e_specs),
        out_specs=list(out_specs),
        out_shape=list(out_shapes),
        scratch_shapes=[] if prologue is _first_input else [pltpu.VMEM((tm, k), BF16)],
        compiler_params=_cparams(2),
        name=name,
    )(*a_arrays, w, *e_arrays)


def _gated_residual_epilogue(rows, tm):
    nsub = tm // rows.sub

    def epi(acc, e_refs, o_refs, i, j):
        res_ref, m_ref = e_refs
        for s in range(nsub):
            idx, _, _ = rows.info(i * nsub + s)
            sl = slice(s * rows.sub, (s + 1) * rows.sub)
            o_refs[0][sl, :] = res_ref[sl, :] + m_ref[pl.ds(idx, 1), :] * acc[sl, :]
    return epi


def _store_epilogue(acc, e_refs, o_refs, i, j):
    o_refs[0][...] = acc.astype(o_refs[0].dtype)


def _first_input(a_refs, i, grp):
    return a_refs[0][...]


def _proj_residual(name, a_arrays, a_specs, prologue, w, res, mods, rows, gate_col, tm, tn):
    m, d = res.shape
    k = w.shape[0]
    gate_blk = gate_col * (d // tn)
    return _mm(name, a_arrays, a_specs, prologue, w,
               [res, mods],
               [pl.BlockSpec((tm, tn), lambda i, j: (i, j)),
                pl.BlockSpec((MOD_ROWS, tn), lambda i, j: (0, gate_blk + j))],
               _gated_residual_epilogue(rows, tm),
               [jax.ShapeDtypeStruct((m, d), F32)],
               [pl.BlockSpec((tm, tn), lambda i, j: (i, j))],
               m, k, d, tm, tn)[0]


def _softplus(z):
    return jnp.maximum(z, 0.0) + jnp.log(1.0 + jnp.exp(-jnp.abs(z)))


NT_DIMS = (((1,), (1,)), ((), ()))
TN_DIMS = (((0,), (0,)), ((), ()))


def _split_dot(x, ones, terms, ones_left=False):
    acc = None
    rem = x
    for _ in range(terms):
        piece = rem.astype(BF16)
        part = (jnp.dot(ones, piece, preferred_element_type=F32) if ones_left
                else jnp.dot(piece, ones, preferred_element_type=F32))
        acc = part if acc is None else acc + part
        rem = rem - piece.astype(F32)
    return acc


def _chunk_kernel(rf, kf, vf, lwf, icf, rr, kr, vr, lwr, icr, kk_ref, ka_ref, yf_ref, yr_ref, s_ref, *, c, n, hb, w):
    j = pl.program_id(2)

    @pl.when(j == 0)
    def _():
        s_ref[...] = jnp.zeros_like(s_ref)

    hp = w // n
    trow = lax.broadcasted_iota(jnp.int32, (c, w), 0)
    lane = lax.broadcasted_iota(jnp.int32, (c, w), 1)
    scol = lane & (c - 1)
    row = lax.broadcasted_iota(jnp.int32, (c, c), 0)
    col = lax.broadcasted_iota(jnp.int32, (c, c), 1)
    lane_head = lane // n
    seg_b = (lax.broadcasted_iota(jnp.int32, (w, w), 0) // n
             == lax.broadcasted_iota(jnp.int32, (w, w), 1) // n)
    seg = seg_b.astype(F32).astype(BF16)
    doublings = int(math.log2(c))

    def bdiag(x):
        return jnp.concatenate([jnp.where(lane_head == h, x, jnp.zeros_like(x)) for h in range(hp)], axis=0)

    streams = ((rf, kf, vf, lwf, icf, yf_ref, col <= row, scol <= trow, scol < trow, c - 1),
               (rr, kr, vr, lwr, icr, yr_ref, col >= row, scol >= trow, scol > trow, 0))
    chains = []
    for dirn, (r_, k_, v_, lw_, ic_, y_, incl, incl_p, strict_p, last) in enumerate(streams):
        tri = incl.astype(F32).astype(BF16)
        for p in range(hb // hp):
            sl = slice(p * w, (p + 1) * w)
            r, k, v, lw, ic = r_[:, sl], k_[:, sl], v_[:, sl], lw_[:, sl], ic_[:, sl]
            kx = k * kk_ref[:, sl]
            nrm = jnp.sqrt(_split_dot(kx * kx, seg, 3))
            kkn = kx / jnp.maximum(nrm, 1e-12)
            g = _split_dot(lw, tri, 3, ones_left=True)
            eig = jnp.exp(-g)
            kq = ((k * (1.0 + (ic - 1.0) * ka_ref[:, sl])) * eig).astype(BF16)
            bq = ((kkn * ic) * eig).astype(BF16)
            chains.append(dict(
                dirn=dirn, p=p, sl=sl, y_ref=y_, incl_p=incl_p, strict_p=strict_p, eg=jnp.exp(g[last:last + 1, :]),
                vb=v.astype(BF16), qm=jnp.concatenate([kq, bq], axis=0),
                pm=jnp.concatenate([r * jnp.exp(g), -kkn * jnp.exp(g - lw)], axis=0).astype(BF16),
                qbd=jnp.concatenate([bdiag(kq), bdiag(bq)], axis=0)))
    for ch in chains:
        gp = lax.dot_general(ch["pm"], ch["qbd"], NT_DIMS, preferred_element_type=F32)
        ch["a_r"] = jnp.concatenate([jnp.where(ch["incl_p"], gp[:c, :w], 0.0),
                                     jnp.where(ch["strict_p"], gp[c:, :w], 0.0)], axis=0).astype(BF16)
        ch["a_rb"] = jnp.where(ch["incl_p"], gp[:c, w:], 0.0).astype(BF16)
        ch["m"] = jnp.where(ch["strict_p"], gp[c:, w:], 0.0)
    for ch in chains:
        ch["s0"] = s_ref[ch["dirn"], ch["p"]]
        ch["ps"] = lax.dot_general(ch["pm"], ch["s0"].astype(BF16), NT_DIMS, preferred_element_type=F32)
    for ch in chains:
        t = ch["ps"] + jnp.dot(ch["a_r"], bdiag(ch["vb"]), preferred_element_type=F32)
        ch["y0"], ch["x"] = t[:c], t[c:]
    for it in range(doublings):
        for ch in chains:
            m, x = ch["m"], ch["x"]
            mb, xb = m.astype(BF16), x.astype(BF16)
            xl = (x - xb.astype(F32)).astype(BF16)
            ch["x"] = x + jnp.dot(jnp.concatenate([mb, mb], axis=1), jnp.concatenate([bdiag(xb), bdiag(xl)], axis=0),
                                  preferred_element_type=F32)
            if it + 1 < doublings:
                ch["m"] = jnp.dot(mb, bdiag(mb), preferred_element_type=F32)
    for ch in chains:
        ub = ch["x"].astype(BF16)
        ch["y_ref"][:, ch["sl"]] = ch["y0"] + jnp.dot(ch["a_rb"], bdiag(ub), preferred_element_type=F32)
        vu = jnp.concatenate([ch["vb"], ub], axis=0)
        upd = lax.dot_general(vu, ch["qm"], TN_DIMS, preferred_element_type=F32)
        s_ref[ch["dirn"], ch["p"]] = (ch["s0"] + jnp.where(seg_b, upd, 0.0)) * ch["eg"]


def _rwkv_chunked(rkv, logw, icl, k_k, k_a, b, lt, n_ctx, h, n):
    d = h * n
    c = CHUNK
    hb = min(CHUNK_HEADS, h)
    slab = min(CHUNK_SLAB, hb * n)
    assert c == n and slab % n == 0 and (hb * n) % slab == 0, "slab layout: chunk length equals the head size"
    nch, nc, ng, w = lt // c, n_ctx // c, h // hb, hb * n

    def rev(j):
        return jnp.where(j < nc, nc - 1 - j, nch - 1 - (j - nc))

    def spec(tmap, colgrp):
        return pl.BlockSpec((c, w), lambda bb, g, j: (bb * nch + tmap(j), colgrp * ng + g))

    ident = lambda j: j
    const = pl.BlockSpec((1, w), lambda bb, g, j: (0, g))
    return pl.pallas_call(
        functools.partial(_chunk_kernel, c=c, n=n, hb=hb, w=slab),
        grid=(b, ng, nch),
        in_specs=[spec(ident, 0), spec(ident, 1), spec(ident, 2), spec(ident, 0), spec(ident, 0),
                  spec(rev, 0), spec(rev, 1), spec(rev, 2), spec(rev, 1), spec(rev, 1), const, const],
        out_specs=[spec(ident, 0), spec(rev, 0)],
        out_shape=[jax.ShapeDtypeStruct((b * lt, d), F32)] * 2,
        scratch_shapes=[pltpu.VMEM((2, hb * n // slab, slab, slab), F32)],
        compiler_params=_cparams(3),
        name="rwkv_chunk",
    )(rkv, rkv, rkv, logw, icl, rkv, rkv, rkv, logw, icl, k_k.reshape(1, d), k_a.reshape(1, d))


def _readout_tok_kernel(yf_ref, yr_ref, r_ref, k_ref, v_ref, a0_ref, a1_ref, g_ref, ka_ref, rk_ref, gw_ref, gb_ref,
                        o_ref, *, n, d):
    seg = (lax.broadcasted_iota(jnp.int32, (LANES, LANES), 0) // n
           == lax.broadcasted_iota(jnp.int32, (LANES, LANES), 1) // n).astype(F32).astype(BF16)

    def seg_sum(t):
        return _split_dot(t, seg, 2)

    for p in range(d // LANES):
        sl = slice(p * LANES, (p + 1) * LANES)
        y = yf_ref[:, sl] + yr_ref[:, sl]
        yc = y - seg_sum(y) * (1.0 / n)
        var = seg_sum(yc * yc) * (1.0 / n)
        yn = (yc * lax.rsqrt(var + GN_EPS)) * gw_ref[:, sl] + gb_ref[:, sl]
        k = k_ref[:, sl]
        ka = ka_ref[:, sl]
        kd = k * (1.0 + (a0_ref[:, sl] - 1.0) * ka) + k * (1.0 + (a1_ref[:, sl] - 1.0) * ka)
        bonus = seg_sum(r_ref[:, sl] * rk_ref[:, sl] * kd)
        o_ref[:, sl] = ((yn + bonus * v_ref[:, sl]) * g_ref[:, sl]).astype(o_ref.dtype)


def _rwkv_readout_tok(yf, yr, rkv, icl, gate, k_a, r_k, gn_w, gn_b, n):
    m, d = yf.shape
    tm = _pick(m, (256, 128, 64))
    blk = lambda cg: pl.BlockSpec((tm, d), lambda i: (i, cg))
    const = pl.BlockSpec((1, d), lambda i: (0, 0))
    return pl.pallas_call(
        functools.partial(_readout_tok_kernel, n=n, d=d),
        grid=(m // tm,),
        in_specs=[blk(0), blk(0), blk(0), blk(1), blk(2), blk(0), blk(1), blk(0)] + [const] * 4,
        out_specs=blk(0),
        out_shape=jax.ShapeDtypeStruct((m, d), BF16),
        compiler_params=_cparams(1),
        name="rwkv_readout",
    )(yf, yr, rkv, rkv, rkv, icl, icl, gate, k_a.reshape(1, d), r_k.reshape(1, d), gn_w.reshape(1, d), gn_b.reshape(1, d))


def _ffn_up_kernel(h_ref, hp_ref, hn_ref, wg_ref, wv_ref, cw_ref, cb_ref, o_ref, lhs_ref, *, rows, tm):
    i = pl.program_id(0)

    @pl.when(pl.program_id(1) == 0)
    def _():
        lhs_ref[0:tm, :] = h_ref[...]
        lhs_ref[tm:tm + BF16_ROWS, :] = hp_ref[...]
        lhs_ref[tm + BF16_ROWS:tm + 2 * BF16_ROWS, :] = hn_ref[...]

    gate_all = jnp.dot(lhs_ref[...], wg_ref[...], preferred_element_type=F32)
    gate = gate_all[:tm]
    gp = gate_all[tm + BF16_ROWS - 1:tm + BF16_ROWS, :]
    gn = gate_all[tm + BF16_ROWS:tm + BF16_ROWS + 1, :]
    val = jnp.dot(h_ref[...], wv_ref[...], preferred_element_type=F32)
    rid, is_start, is_end = _edge_masks(rows, i, tm)
    prev, nxt = _shifted(gate, gp, gn, rid, is_start, is_end, tm)
    conv = prev * cw_ref[0:1, :] + gate * cw_ref[1:2, :] + nxt * cw_ref[2:3, :] + cb_ref[...]
    o_ref[...] = (_silu(conv) * val).astype(o_ref.dtype)


def _ffn_up(hf, wup, conv_w, conv_b, rows, tm, tn):
    m, d = hf.shape
    f = wup.shape[1] // 2
    r8 = tm // BF16_ROWS
    nblk8 = m // BF16_ROWS
    njf = f // tn
    cw = jnp.zeros((SUBLANES, f), F32).at[:conv_w.shape[0]].set(conv_w)
    return pl.pallas_call(
        functools.partial(_ffn_up_kernel, rows=rows, tm=tm),
        grid=(m // tm, njf),
        in_specs=[pl.BlockSpec((tm, d), lambda i, j: (i, 0)),
                  pl.BlockSpec((BF16_ROWS, d), lambda i, j: (jnp.maximum(i * r8 - 1, 0), 0)),
                  pl.BlockSpec((BF16_ROWS, d), lambda i, j: (jnp.minimum((i + 1) * r8, nblk8 - 1), 0)),
                  pl.BlockSpec((d, tn), lambda i, j: (0, j)),
                  pl.BlockSpec((d, tn), lambda i, j: (0, j + njf)),
                  pl.BlockSpec((SUBLANES, tn), lambda i, j: (0, j)),
                  pl.BlockSpec((1, tn), lambda i, j: (0, j))],
        out_specs=pl.BlockSpec((tm, tn), lambda i, j: (i, j)),
        out_shape=jax.ShapeDtypeStruct((m, f), BF16),
        scratch_shapes=[pltpu.VMEM((tm + 2 * BF16_ROWS, d), BF16)],
        compiler_params=_cparams(2),
        name="ffn_up",
    )(hf, hf, hf, wup, wup, cw, conv_b.reshape(1, f))


def _conv_ffn(x2, g, mods, rows, wup, conv_w, conv_b, wdown, tm):
    m, d = x2.shape
    f = wdown.shape[0]
    hf = _norm_mod(x2, g, mods, rows, 3, 4, BF16)
    act = _ffn_up(hf, wup.astype(BF16), conv_w, conv_b, rows, tm, _pick(f, (512, 256, 128)))
    tmd = _pick(m, tuple(t for t in (1024, 512, 256, 128, 64) if t <= tm and t % rows.sub == 0))
    return _proj_residual("ffn_down", [act], [pl.BlockSpec((tmd, f), lambda i, j: (i, 0))],
                          _first_input, wdown.astype(BF16), x2, mods, rows, 5,
                          tmd, _pick(d, (512, 256, 128)))


def _rope(x, cos, sin_signed):
    lane = lax.broadcasted_iota(jnp.int32, x.shape, 1)
    first = (lane & 31) < 16
    swapped = jnp.where(first, pltpu.roll(x, LANES - 16, 1), pltpu.roll(x, 16, 1))
    return x * cos + swapped * sin_signed


def _kv_up_kernel(c_ref, wk_ref, wv_ref, kr_ref, ko_ref, vo_ref, *, hpt):
    c = c_ref[...]
    kn = jnp.dot(c, wk_ref[...], preferred_element_type=F32).astype(BF16)
    vo_ref[...] = jnp.dot(c, wv_ref[...], preferred_element_type=F32).astype(BF16)
    kr = kr_ref[...]
    for h in range(hpt):
        ko_ref[:, h * 2 * LANES:h * 2 * LANES + QK_NOPE] = kn[:, h * QK_NOPE:(h + 1) * QK_NOPE]
        ko_ref[:, h * 2 * LANES + QK_NOPE:(h + 1) * 2 * LANES] = kr


def _attn_kernel(q_ref, k_ref, v_ref, o_ref, *, hpa, hq):
    ss = [lax.dot_general(q_ref[:, h * hq:(h + 1) * hq], k_ref[:, h * hq:(h + 1) * hq], NT_DIMS,
                          preferred_element_type=F32) for h in range(hpa)]
    ps = [jnp.exp2(s - jnp.max(s, axis=-1, keepdims=True)) for s in ss]
    for h, p in enumerate(ps):
        denom = jnp.sum(p, axis=-1, keepdims=True)
        o = jnp.dot(p.astype(BF16), v_ref[:, h * V_HEAD:(h + 1) * V_HEAD], preferred_element_type=F32)
        o_ref[:, h * V_HEAD:(h + 1) * V_HEAD] = (o / denom).astype(o_ref.dtype)


def _rwkv_layer(xa, mods, norm_g, rows, p, tm):
    (mu, wr, wk, wv, wo, w0, w1, w2, a0, a1, a2, g1, g2, k_k, k_a, r_k, gn_w, gn_b) = p
    m, d = xa.shape
    h_heads, n = r_k.shape
    b, lt = rows.batch, rows.per_batch
    glora, dlora, alora = g1.shape[1], w1.shape[2], a1.shape[2]
    gw = _round_up(max(glora, 2 * dlora, 2 * alora), LANES)

    h, xx = _rwkv_prep(xa, norm_g, mods, rows)

    a_specs = [pl.BlockSpec((tm, d), lambda i, j: (i, 0)), pl.BlockSpec((tm, d), lambda i, j: (i, 0)),
               pl.BlockSpec((MOD_ROWS, d), lambda i, j: (0, 0))]

    def mix(a, i, grp):
        return (a[0][...] + a[1][...] * a[2][pl.ds(grp, 1), :]).astype(BF16)

    def pad_rows(t):
        return jnp.zeros((MOD_ROWS, d), F32).at[:t.shape[0]].set(t)

    tn = _pick(d, (512, 256, 128))
    w_rkv = jnp.concatenate([wr, wk, wv], axis=1).astype(BF16)
    rkv = _mm("rwkv_rkv", [h, xx, pad_rows(mu[jnp.array([0, 2, 3])])], a_specs, mix, w_rkv, [], [],
              _store_epilogue,
              [jax.ShapeDtypeStruct((m, 3 * d), F32)], [pl.BlockSpec((tm, tn), lambda i, j: (i, j))],
              m, d, 3 * d, tm, tn, period=d // tn)[0]

    def pad_cols(t):
        return jnp.pad(t, ((0, 0), (0, gw - t.shape[1])))

    w_lora = jnp.concatenate([pad_cols(g1), pad_cols(jnp.concatenate([w1[0], w1[1]], axis=1)),
                              pad_cols(jnp.concatenate([a1[0], a1[1]], axis=1))], axis=1).astype(BF16)

    def lora_act(acc, e, o, i, j):
        o[0][...] = jnp.where(j == 0, jax.nn.sigmoid(acc), jnp.where(j == 1, jnp.tanh(acc), acc)).astype(BF16)

    hid = _mm("rwkv_lora1", [h, xx, pad_rows(mu[jnp.array([5, 1, 4])])], a_specs, mix, w_lora, [], [], lora_act,
              [jax.ShapeDtypeStruct((m, 3 * gw), BF16)], [pl.BlockSpec((tm, gw), lambda i, j: (i, j))],
              m, d, 3 * gw, tm, gw, period=1)[0]

    def hid_spec(grp):
        return [pl.BlockSpec((tm, gw), lambda i, j: (i, grp))]

    ident = _first_input
    g2p = jnp.zeros((gw, d), F32).at[:glora].set(g2).astype(BF16)
    gate = _mm("rwkv_gate", [hid], hid_spec(0), ident, g2p, [], [],
               _store_epilogue,
               [jax.ShapeDtypeStruct((m, d), F32)], [pl.BlockSpec((tm, tn), lambda i, j: (i, j))],
               m, gw, d, tm, tn)[0]

    def two_dir(t2, lora):
        z = jnp.zeros((gw, 2 * d), F32)
        z = z.at[:lora, :d].set(t2[0]).at[lora:2 * lora, d:].set(t2[1])
        return z.astype(BF16)

    bias_spec = [pl.BlockSpec((1, tn), lambda i, j: (0, j))]

    def decay_epi(acc, e, o, i, j):
        w_log = -_softplus(-(e[0][...] + acc)) - 0.5
        o[0][...] = -jnp.exp(w_log)

    logw = _mm("rwkv_decay", [hid], hid_spec(1), ident, two_dir(w2, dlora), [w0.reshape(1, 2 * d)], bias_spec,
               decay_epi, [jax.ShapeDtypeStruct((m, 2 * d), F32)], [pl.BlockSpec((tm, tn), lambda i, j: (i, j))],
               m, gw, 2 * d, tm, tn)[0]

    def iclr_epi(acc, e, o, i, j):
        o[0][...] = jax.nn.sigmoid(e[0][...] + acc)

    icl = _mm("rwkv_iclr", [hid], hid_spec(2), ident, two_dir(a2, alora), [a0.reshape(1, 2 * d)], bias_spec,
              iclr_epi, [jax.ShapeDtypeStruct((m, 2 * d), F32)], [pl.BlockSpec((tm, tn), lambda i, j: (i, j))],
              m, gw, 2 * d, tm, tn)[0]

    yf, yr = _rwkv_chunked(rkv, logw, icl, k_k, k_a, b, lt, rows.n_ctx, h_heads, n)
    og = _rwkv_readout_tok(yf, yr, rkv, icl, gate, k_a, r_k.reshape(d), gn_w, gn_b, n)
    return _proj_residual("rwkv_out", [og], [pl.BlockSpec((tm, d), lambda i, j: (i, 0))], _first_input,
                          wo.astype(BF16), xa, mods, rows, 2, tm, tn)


def _mla_layer(xa, xl, mods, norm_g, rows_u, rows_l, p, tm_u, tm_l):
    wdown, qnorm, kvnorm, wuq, wukv, wo = p
    m, d = xa.shape
    b, lt, lc, ll = rows_u.batch, rows_u.per_batch, rows_u.n_ctx, rows_u.n_lat
    ql, kvl = qnorm.shape[0], kvnorm.shape[0]
    heads = wuq.shape[1] // (QK_NOPE + QK_ROPE)
    hq = 2 * LANES

    npairs = QK_ROPE // 4
    pos = jnp.arange(ll)
    inv_freq = jnp.float32(ROPE_THETA) ** (-jnp.arange(npairs, dtype=F32) / npairs)
    ang_r = (pos // GRID_W).astype(F32)[:, None] * inv_freq
    ang_c = (pos % GRID_W).astype(F32)[:, None] * inv_freq
    cos64 = jnp.concatenate([jnp.cos(ang_r)] * 2 + [jnp.cos(ang_c)] * 2, axis=1)
    sin64 = jnp.concatenate([-jnp.sin(ang_r), jnp.sin(ang_r), -jnp.sin(ang_c), jnp.sin(ang_c)], axis=1)
    padl = ((lc, 0), (0, LANES - QK_ROPE))
    cos_t = jnp.tile(jnp.pad(cos64, padl, constant_values=1.0), (b, 1))
    sin_t = jnp.tile(jnp.pad(sin64, padl), (b, 1))

    h1 = _norm_mod(xa, norm_g, mods, rows_u, 0, 1, BF16)

    nd = _round_up(ql + kvl + QK_ROPE, LANES)
    wd = jnp.pad(wdown, ((0, 0), (0, nd - wdown.shape[1]))).astype(BF16)

    def down_epi(acc, e, o, i, j):
        qn_ref, kvn_ref, cos_ref, sin_ref = e
        o[0][...] = (_rms(acc[:, :ql]) * qn_ref[...]).astype(BF16)
        o[1][...] = (_rms(acc[:, ql:ql + kvl]) * kvn_ref[...]).astype(BF16)
        o[2][...] = _rope(acc[:, ql + kvl:], cos_ref[...], sin_ref[...]).astype(BF16)

    tab_spec = pl.BlockSpec((tm_u, LANES), lambda i, j: (i, 0))
    cq, ckv, krope = _mm(
        "mla_down", [h1], [pl.BlockSpec((tm_u, d), lambda i, j: (i, 0))], _first_input, wd,
        [qnorm.reshape(1, ql), kvnorm.reshape(1, kvl), cos_t, sin_t],
        [pl.BlockSpec((1, ql), lambda i, j: (0, 0)), pl.BlockSpec((1, kvl), lambda i, j: (0, 0)), tab_spec, tab_spec],
        down_epi,
        [jax.ShapeDtypeStruct((m, ql), BF16), jax.ShapeDtypeStruct((m, kvl), BF16),
         jax.ShapeDtypeStruct((m, LANES), BF16)],
        [pl.BlockSpec((tm_u, ql), lambda i, j: (i, 0)), pl.BlockSpec((tm_u, kvl), lambda i, j: (i, 0)),
         pl.BlockSpec((tm_u, LANES), lambda i, j: (i, 0))],
        m, d, nd, tm_u, nd)

    wq = wuq.reshape(ql, heads, QK_NOPE + QK_ROPE)
    wq = jnp.pad(wq, ((0, 0), (0, 0), (0, hq - QK_NOPE - QK_ROPE))).reshape(ql, heads * hq).astype(BF16)
    hpt = _pick(heads, (4, 2, 1))

    qscale = (QK_NOPE + QK_ROPE) ** -0.5 * math.log2(math.e)

    def q_epi(acc, e, o, i, j):
        cos_ref, sin_ref = e
        acc = acc * qscale
        for hh in range(hpt):
            lo = hh * hq
            o[0][:, lo:lo + QK_NOPE] = acc[:, lo:lo + QK_NOPE].astype(BF16)
            o[0][:, lo + QK_NOPE:lo + hq] = _rope(acc[:, lo + QK_NOPE:lo + hq], cos_ref[...], sin_ref[...]).astype(BF16)

    q = _mm("mla_q_up", [cq], [pl.BlockSpec((tm_u, ql), lambda i, j: (i, 0))], _first_input, wq,
            [cos_t, sin_t], [tab_spec, tab_spec], q_epi,
            [jax.ShapeDtypeStruct((m, heads * hq), BF16)], [pl.BlockSpec((tm_u, hpt * hq), lambda i, j: (i, j))],
            m, ql, heads * hq, tm_u, hpt * hq)[0]

    wkv = wukv.reshape(kvl, heads, QK_NOPE + V_HEAD)
    wk = wkv[:, :, :QK_NOPE].reshape(kvl, heads * QK_NOPE).astype(BF16)
    wv = wkv[:, :, QK_NOPE:].reshape(kvl, heads * V_HEAD).astype(BF16)
    kcat, vals = pl.pallas_call(
        functools.partial(_kv_up_kernel, hpt=hpt),
        grid=(m // tm_u, heads // hpt),
        in_specs=[pl.BlockSpec((tm_u, kvl), lambda i, j: (i, 0)),
                  pl.BlockSpec((kvl, hpt * QK_NOPE), lambda i, j: (0, j)),
                  pl.BlockSpec((kvl, hpt * V_HEAD), lambda i, j: (0, j)),
                  pl.BlockSpec((tm_u, LANES), lambda i, j: (i, 0))],
        out_specs=[pl.BlockSpec((tm_u, hpt * hq), lambda i, j: (i, j)),
                   pl.BlockSpec((tm_u, hpt * V_HEAD), lambda i, j: (i, j))],
        out_shape=[jax.ShapeDtypeStruct((m, heads * hq), BF16), jax.ShapeDtypeStruct((m, heads * V_HEAD), BF16)],
        compiler_params=_cparams(2),
        name="mla_kv_up",
    )(ckv, wk, wv, krope)

    tq = _pick(math.gcd(lc, ll), (256, 128, 64))
    hpa = _pick(heads, (2, 1))
    o = pl.pallas_call(
        functools.partial(_attn_kernel, hpa=hpa, hq=hq),
        grid=(b, heads // hpa, ll // tq),
        in_specs=[pl.BlockSpec((None, tq, hpa * hq), lambda bb, hh, i: (bb, lc // tq + i, hh)),
                  pl.BlockSpec((None, lt, hpa * hq), lambda bb, hh, i: (bb, 0, hh)),
                  pl.BlockSpec((None, lt, hpa * V_HEAD), lambda bb, hh, i: (bb, 0, hh))],
        out_specs=pl.BlockSpec((None, tq, hpa * V_HEAD), lambda bb, hh, i: (bb, i, hh)),
        out_shape=jax.ShapeDtypeStruct((b, ll, heads * V_HEAD), BF16),
        compiler_params=_cparams(3),
        name="mla_attn",
    )(q.reshape(b, lt, heads * hq), kcat.reshape(b, lt, heads * hq), vals.reshape(b, lt, heads * V_HEAD))

    ko = heads * V_HEAD
    return _proj_residual("mla_out", [o.reshape(b * ll, ko)], [pl.BlockSpec((tm_l, ko), lambda i, j: (i, 0))],
                          _first_input, wo.astype(BF16), xl, mods, rows_l, 2,
                          tm_l, _pick(d, (512, 256, 128)))


def kernel(x, c, ctx, c_ctx, ada_w, ada_b, norm_g, final_g, rk_mu, rk_wr, rk_wk, rk_wv, rk_wo, rk_w0, rk_w1, rk_w2, rk_a0, rk_a1, rk_a2, rk_g1, rk_g2, rk_kk, rk_ka, rk_rk, rk_gn_w, rk_gn_b, ml_wdown, ml_qnorm, ml_kvnorm, ml_wuq, ml_wukv, ml_wo, ff_wup, ff_conv, ff_convb, ff_wdown):
    b, ll, d = x.shape
    lc = ctx.shape[1]
    depth = ada_w.shape[0]
    heads_r = rk_rk.shape[1]
    assert depth == 2 and rk_mu.shape[0] == 1 and ml_wdown.shape[0] == 1, "one RWKV layer then one MLA layer"
    assert b + 1 <= MOD_ROWS and b * heads_r <= LANES
    sub = _pick(math.gcd(lc, ll), (256, 128, 64, 32, 16, 8))
    rows_u = _Rows(b, lc, ll, sub)
    rows_l = _Rows(b, 0, ll, sub)
    tm_u = _pick(rows_u.total, tuple(t for t in (1024, 512, 256, 128, 64) if t % sub == 0))
    tm_l = _pick(rows_l.total, tuple(t for t in (1024, 512, 256, 128, 64) if t % sub == 0))

    cvec = jnp.zeros((MOD_ROWS, d), F32).at[:b].set(c).at[b].set(c_ctx)
    mods = _ada(cvec, ada_w, ada_b)

    xa = jnp.concatenate([ctx, x], axis=1).reshape(rows_u.total, d)

    p0 = (rk_mu[0], rk_wr[0], rk_wk[0], rk_wv[0], rk_wo[0], rk_w0[0], rk_w1[0], rk_w2[0], rk_a0[0], rk_a1[0],
          rk_a2[0], rk_g1[0], rk_g2[0], rk_kk[0], rk_ka[0], rk_rk[0], rk_gn_w[0], rk_gn_b[0])
    xa = _rwkv_layer(xa, mods[0], norm_g[0, 0], rows_u, p0, tm_u)
    xa = _conv_ffn(xa, norm_g[0, 1], mods[0], rows_u, ff_wup[0], ff_conv[0], ff_convb[0], ff_wdown[0], tm_u)

    xl = xa.reshape(b, rows_u.per_batch, d)[:, lc:].reshape(rows_l.total, d)
    p1 = (ml_wdown[0], ml_qnorm[0], ml_kvnorm[0], ml_wuq[0], ml_wukv[0], ml_wo[0])
    xl = _mla_layer(xa, xl, mods[1], norm_g[1, 0], rows_u, rows_l, p1, tm_u, tm_l)
    xl = _conv_ffn(xl, norm_g[1, 1], mods[1], rows_l, ff_wup[1], ff_conv[1], ff_convb[1], ff_wdown[1], tm_l)
    return _final_norm(xl, final_g).reshape(b, ll, d)
```
